```python
import jax, jax.numpy as jnp
from jax import lax
import numpy as np

D_MODEL = 2048
BATCH = 4
SEQ = 4096
DEPTH = 1

D_MIX = D_MODEL
D_ATTN = D_MIX // 2
D_POOL = D_MIX - D_ATTN
HEAD_DIM = 128
N_HEADS = D_ATTN // HEAD_DIM
ROPE_DIM = HEAD_DIM // 4
ROPE_THETA = 500000.0
DILATED_PATTERNS = ((128, 1), (512, 4), (2048, 16))
POOL_WINDOWS = (2, 4, 8, 16)
N_POOL_GROUPS = len(POOL_WINDOWS)
POOL_GROUP_DIM = D_POOL // N_POOL_GROUPS
D_IN = 3 * D_ATTN + D_POOL + D_MIX
LN_EPS = 1e-5
DEEPNORM_ALPHA = (2.0 * DEPTH) ** 0.25
DEEPNORM_BETA = (8.0 * DEPTH) ** -0.25

kernel_name = "hybrid_dilated_attn_pool_deepnorm"


def rotary_partial(t, positions):
    half = ROPE_DIM // 2
    inv_freq = ROPE_THETA ** (-(2.0 * jnp.arange(half, dtype=jnp.float32)) / ROPE_DIM)
    ang = positions.astype(jnp.float32)[:, None] * inv_freq[None, :]
    cos = jnp.cos(ang)[None, :, None, :]
    sin = jnp.sin(ang)[None, :, None, :]
    t32 = t.astype(jnp.float32)
    t1, t2, rest = t32[..., :half], t32[..., half:ROPE_DIM], t32[..., ROPE_DIM:]
    out = jnp.concatenate([t1 * cos - t2 * sin, t1 * sin + t2 * cos, rest], axis=-1)
    return out.astype(t.dtype)


def banded_causal_attention(q, k, v, n_keys):
    G, N, Dh = q.shape
    W = n_keys
    nb = -(-N // W)
    Np = nb * W
    pad = Np - N
    qp = jnp.pad(q, ((0, 0), (0, pad), (0, 0)))
    kp = jnp.pad(k, ((0, 0), (W, pad), (0, 0)))
    vp = jnp.pad(v, ((0, 0), (W, pad), (0, 0)))
    qb = qp.reshape(G, nb, W, Dh)
    kb = jnp.concatenate([kp[:, :Np].reshape(G, nb, W, Dh), kp[:, W:].reshape(G, nb, W, Dh)], axis=2)
    vb = jnp.concatenate([vp[:, :Np].reshape(G, nb, W, Dh), vp[:, W:].reshape(G, nb, W, Dh)], axis=2)
    s = jnp.einsum('gcqd,gckd->gcqk', qb, kb).astype(jnp.float32) * (Dh ** -0.5)
    qi = jnp.arange(W)[:, None]
    kj = jnp.arange(2 * W)[None, :]
    dist = W + qi - kj
    key_pos = (jnp.arange(nb)[:, None, None] - 1) * W + kj[None]
    mask = (dist >= 0)[None] & (dist <= W)[None] & (key_pos >= 0)
    s = jnp.where(mask[None], s, -jnp.inf)
    m = jnp.max(s, axis=-1, keepdims=True)
    p = jnp.exp(s - m)
    den = jnp.sum(p, axis=-1, keepdims=True)
    o = jnp.einsum('gcqk,gckd->gcqd', p, vb.astype(jnp.float32)) / den
    lse = (m + jnp.log(den))[..., 0]
    return o.reshape(G, Np, Dh)[:, :N], lse.reshape(G, Np)[:, :N]


def dilated_attention(q, k, v, window, dilation):
    B, H, S, Dh = q.shape
    n_sub = S // dilation

    def to_residues(t):
        return t.reshape(B, H, n_sub, dilation, Dh).transpose(0, 1, 3, 2, 4).reshape(B * H * dilation, n_sub, Dh)

    o, lse = banded_causal_attention(to_residues(q), to_residues(k), to_residues(v), window // dilation)
    o = o.reshape(B, H, dilation, n_sub, Dh).transpose(0, 1, 3, 2, 4).reshape(B, H, S, Dh)
    lse = lse.reshape(B, H, dilation, n_sub).transpose(0, 1, 3, 2).reshape(B, H, S)
    return o, lse


def dilated_attention_mixture(q, k, v):
    outs, lses = [], []
    for window, dilation in DILATED_PATTERNS:
        o, lse = dilated_attention(q, k, v, window, dilation)
        outs.append(o)
        lses.append(lse)
    w = jax.nn.softmax(jnp.stack(lses, axis=0), axis=0)
    return jnp.einsum('pbhs,pbhsd->bhsd', w, jnp.stack(outs, axis=0))


def causal_pool_minus_identity(u, window):
    S = u.shape[1]
    u32 = u.astype(jnp.float32)
    c = jnp.cumsum(u32, axis=1)
    c_shift = jnp.pad(c, ((0, 0), (window, 0), (0, 0)))[:, :S]
    count = jnp.minimum(jnp.arange(S) + 1, window).astype(jnp.float32)
    return (c - c_shift) / count[None, :, None] - u32


def layer_norm(h, gain, bias):
    h32 = h.astype(jnp.float32)
    mu = jnp.mean(h32, axis=-1, keepdims=True)
    var = jnp.mean(jnp.square(h32 - mu), axis=-1, keepdims=True)
    return (h32 - mu) * lax.rsqrt(var + LN_EPS) * gain.astype(jnp.float32) + bias.astype(jnp.float32)


def setup_inputs(seed: int = 0) -> dict:
    key = jax.random.key(seed)
    ks = jax.random.split(key, 8)
    x = jax.random.normal(ks[0], (BATCH, SEQ, D_MODEL), jnp.float32)
    w_in = jax.random.normal(ks[1], (DEPTH, D_MODEL, D_IN), jnp.float32) * D_MODEL ** -0.5
    w_pool = jax.random.normal(ks[2], (DEPTH, N_POOL_GROUPS, POOL_GROUP_DIM, POOL_GROUP_DIM), jnp.float32) * POOL_GROUP_DIM ** -0.5
    pool_scale = 1.0 + 0.02 * jax.random.normal(ks[3], (DEPTH, D_POOL), jnp.float32)
    w_out = jax.random.normal(ks[4], (DEPTH, D_MIX, D_MODEL), jnp.float32) * (D_MIX ** -0.5) * DEEPNORM_BETA
    ln_gain = 1.0 + 0.02 * jax.random.normal(ks[5], (DEPTH, D_MODEL), jnp.float32)
    ln_bias = 0.02 * jax.random.normal(ks[6], (DEPTH, D_MODEL), jnp.float32)
    return {"x": x, "w_in": w_in, "w_pool": w_pool, "pool_scale": pool_scale,
            "w_out": w_out, "ln_gain": ln_gain, "ln_bias": ln_bias}


def reference(x, w_in, w_pool, pool_scale, w_out, ln_gain, ln_bias):
    B, S, _ = x.shape
    positions = jnp.arange(S, dtype=jnp.int32)
    for layer in range(DEPTH):
        h = jnp.einsum('bsd,de->bse', x, w_in[layer])
        q, k, v, u_pool, gate = jnp.split(
            h, [D_ATTN, 2 * D_ATTN, 3 * D_ATTN, 3 * D_ATTN + D_POOL], axis=-1)

        q = rotary_partial(q.reshape(B, S, N_HEADS, HEAD_DIM), positions).transpose(0, 2, 1, 3)
        k = rotary_partial(k.reshape(B, S, N_HEADS, HEAD_DIM), positions).transpose(0, 2, 1, 3)
        v = v.reshape(B, S, N_HEADS, HEAD_DIM).transpose(0, 2, 1, 3)
        attn = dilated_attention_mixture(q, k, v)
        attn = attn.transpose(0, 2, 1, 3).reshape(B, S, D_ATTN)

        u_groups = u_pool.reshape(B, S, N_POOL_GROUPS, POOL_GROUP_DIM)
        pooled = jnp.stack([causal_pool_minus_identity(u_groups[:, :, g], POOL_WINDOWS[g])
                            for g in range(N_POOL_GROUPS)], axis=2)
        pool_out = jnp.einsum('bsgc,gcd->bsgd', pooled, w_pool[layer].astype(jnp.float32))
        pool_out = pool_out.reshape(B, S, D_POOL) * pool_scale[layer].astype(jnp.float32)

        y = jnp.concatenate([attn, pool_out], axis=-1) * jax.nn.silu(gate.astype(jnp.float32))
        out = jnp.einsum('bse,ed->bsd', y.astype(x.dtype), w_out[layer])

        x = layer_norm(DEEPNORM_ALPHA * x.astype(jnp.float32) + out.astype(jnp.float32),
                       ln_gain[layer], ln_bias[layer]).astype(x.dtype)
    return x
```

```python
import functools

import jax
import jax.numpy as jnp
from jax import lax
from jax.experimental import pallas as pl
from jax.experimental.pallas import tpu as pltpu

D_MODEL = 2048
D_ATTN = 1024
D_POOL = 1024
HEAD_DIM = 128
N_HEADS = D_ATTN // HEAD_DIM
ROPE_DIM = HEAD_DIM // 4
ROPE_HALF = ROPE_DIM // 2
ROPE_THETA = 500000.0
N_KEYS = 128
DILATIONS = (1, 4, 16)
MAX_DIL = 16
POOL_WINDOWS = (2, 4, 8, 16)
POOL_GROUP_DIM = D_POOL // len(POOL_WINDOWS)
POOL_HALO = 16
D_IN = 3 * D_ATTN + D_POOL + D_MODEL
LN_EPS = 1e-5
DEEPNORM_ALPHA = 2.0 ** 0.25

LANES = 128
VMEM_LIMIT = 56 * 1024 * 1024

PROJ_TM = 512
PROJ_TN = 512
OUT_TM = 512

BF16 = jnp.bfloat16
F32 = jnp.float32


def _proj_kernel(x_ref, w_ref, rot_ref, nat_ref, r16_ref, xb_ref, slab_ref, *, tm, tn):
    j = pl.program_id(1)
    heads = tn // LANES
    n_q = D_ATTN // tn
    n_qkv = 3 * n_q
    n_pool = D_POOL // tn

    @pl.when(j == 0)
    def _():
        xb_ref[...] = x_ref[...].astype(BF16)

    def matmul():
        return jnp.dot(xb_ref[...], w_ref[...], preferred_element_type=F32)

    def write_both(vals):
        for h, t in enumerate(vals):
            nat_ref[:, h * LANES:(h + 1) * LANES] = t.astype(BF16)
            slab_ref[h] = t
        rows = tm // MAX_DIL
        for r in range(MAX_DIL):
            for h in range(heads):
                t = slab_ref[h, pl.ds(r, rows, stride=MAX_DIL), :]
                r16_ref[r, :, h * LANES:(h + 1) * LANES] = t.astype(BF16)

    @pl.when(j < 2 * n_q)
    def _():
        res = matmul()
        c, a, b = rot_ref[0], rot_ref[1], rot_ref[2]
        vals = []
        for h in range(heads):
            t = res[:, h * LANES:(h + 1) * LANES]
            vals.append(t * c + pltpu.roll(t, ROPE_HALF, 1) * a
                        + pltpu.roll(t, LANES - ROPE_HALF, 1) * b)
        write_both(vals)

    @pl.when((j >= 2 * n_q) & (j < n_qkv))
    def _():
        res = matmul()
        write_both([res[:, h * LANES:(h + 1) * LANES] for h in range(heads)])

    @pl.when((j >= n_qkv) & (j < n_qkv + n_pool))
    def _():
        nat_ref[...] = matmul().astype(BF16)

    @pl.when(j >= n_qkv + n_pool)
    def _():
        g = matmul()
        nat_ref[...] = (g / (1.0 + jnp.exp(-g))).astype(BF16)


def _projection(x2, w_in_b, rot, batch, seq):
    tm, tn = PROJ_TM, PROJ_TN
    m = x2.shape[0]
    tiles_per_seq = seq // tm
    n_q = D_ATTN // tn
    last_qkv = 3 * n_q - 1
    grid = (m // tm, D_IN // tn)
    kern = functools.partial(_proj_kernel, tm=tm, tn=tn)
    return pl.pallas_call(
        kern,
        grid=grid,
        in_specs=[
            pl.BlockSpec((tm, D_MODEL), lambda i, j: (i, 0)),
            pl.BlockSpec((D_MODEL, tn), lambda i, j: (0, j)),
            pl.BlockSpec((None, 3, tm, LANES),
                         lambda i, j: (jnp.minimum(j // n_q, 1), 0, i % tiles_per_seq, 0)),
        ],
        out_specs=[
            pl.BlockSpec((tm, tn), lambda i, j: (i, j)),
            pl.BlockSpec((None, MAX_DIL, tm // MAX_DIL, tn),
                         lambda i, j: (i // tiles_per_seq, 0, i % tiles_per_seq,
                                       jnp.minimum(j, last_qkv))),
        ],
        out_shape=[
            jax.ShapeDtypeStruct((m, D_IN), BF16),
            jax.ShapeDtypeStruct((batch, MAX_DIL, seq // MAX_DIL, 3 * D_ATTN), BF16),
        ],
        scratch_shapes=[
            pltpu.VMEM((tm, D_MODEL), BF16),
            pltpu.VMEM((tn // LANES, tm, LANES), F32),
        ],
        compiler_params=pltpu.CompilerParams(
            dimension_semantics=("arbitrary", "arbitrary"),
            vmem_limit_bytes=VMEM_LIMIT),
        name="in_projection",
    )(x2, w_in_b, rot)


def _attend(q, k, v, bias):
    s = lax.dot_general(q, k, (((1,), (1,)), ((), ())), preferred_element_type=F32) + bias
    m = jnp.max(s, axis=1, keepdims=True)
    p = jnp.exp(s - m)
    den = jnp.sum(p, axis=1, keepdims=True)
    acc = jnp.dot(p.astype(BF16), v, preferred_element_type=F32)
    return acc / den, m + jnp.log(den)


def _attn_kernel(q_ref, k_ref, v_ref, g_ref, q16_ref, k16_ref, v16_ref, bias_ref, out_ref,
                 o1_ref, l1_ref, o16_ref, l16_ref, onat_ref, *, seq):
    W = N_KEYS
    n_sub16 = seq // 16
    n_sub4 = seq // 4
    blocks1 = seq // W
    blocks4 = n_sub4 // W
    blocks16 = n_sub16 // W
    chunk4 = W // 4
    wide = (W, LANES)

    bias_full = bias_ref[0]
    bias_first = bias_ref[1, :, :W]
    bias4_full = bias_ref[2]
    bias4_first = bias_ref[3, :, :W]

    def d1_store(j, o, lse):
        rows = pl.ds(pl.multiple_of(j * W, W), W)
        o1_ref[rows, :] = o
        l1_ref[rows, :] = jnp.broadcast_to(lse, wide)

    o, lse = _attend(q_ref[0:W, :], k_ref[0:W, :], v_ref[0:W, :], bias_first)
    d1_store(0, o, lse)

    def d1_body(j, carry):
        qrows = pl.ds(pl.multiple_of(j * W, W), W)
        krows = pl.ds(pl.multiple_of((j - 1) * W, W), 2 * W)
        o, lse = _attend(q_ref[qrows, :], k_ref[krows, :], v_ref[krows, :], bias_full)
        d1_store(j, o, lse)
        return carry

    lax.fori_loop(1, blocks1, d1_body, 0)

    def d16_body(r, carry):
        base = pl.multiple_of(r * n_sub16, n_sub16)
        o, lse = _attend(q16_ref[r, 0:W, :], k16_ref[r, 0:W, :], v16_ref[r, 0:W, :], bias_first)
        o16_ref[pl.ds(base, W), :] = o
        l16_ref[pl.ds(base, W), :] = jnp.broadcast_to(lse, wide)
        for jb in range(1, blocks16):
            o, lse = _attend(q16_ref[r, jb * W:(jb + 1) * W, :],
                             k16_ref[r, (jb - 1) * W:(jb + 1) * W, :],
                             v16_ref[r, (jb - 1) * W:(jb + 1) * W, :], bias_full)
            o16_ref[pl.ds(base + jb * W, W), :] = o
            l16_ref[pl.ds(base + jb * W, W), :] = jnp.broadcast_to(lse, wide)
        return carry

    lax.fori_loop(0, MAX_DIL, d16_body, 0)

    def d4_merge_store(r4, j, o4, lse4):
        for c in range(4):
            cls = 4 * c + r4
            rows16 = pl.ds(pl.multiple_of(cls * n_sub16 + j * chunk4, chunk4), chunk4)
            rows_nat = pl.ds(j * chunk4 * MAX_DIL + cls, chunk4, stride=MAX_DIL)
            oc4 = o4[c * chunk4:(c + 1) * chunk4, :]
            lc4 = jnp.broadcast_to(lse4[c * chunk4:(c + 1) * chunk4, :], (chunk4, LANES))
            oc16, lc16 = o16_ref[rows16, :], l16_ref[rows16, :]
            oc1, lc1 = o1_ref[rows_nat, :], l1_ref[rows_nat, :]
            mx = jnp.maximum(jnp.maximum(lc1, lc4), lc16)
            w1, w4, w16 = jnp.exp(lc1 - mx), jnp.exp(lc4 - mx), jnp.exp(lc16 - mx)
            mixed = (w1 * oc1 + w4 * oc4 + w16 * oc16) / (w1 + w4 + w16)
            onat_ref[rows_nat, :] = mixed

    def d4_class(r4, carry):
        def gather(ref, start, size):
            return jnp.concatenate(
                [ref[4 * c + r4, pl.ds(start, size), :] for c in range(4)], axis=0)

        o4, lse4 = _attend(gather(q16_ref, 0, chunk4), gather(k16_ref, 0, chunk4),
                           gather(v16_ref, 0, chunk4), bias4_first)
        d4_merge_store(r4, 0, o4, lse4)

        def d4_body(j, carry2):
            qs = pl.multiple_of(j * chunk4, chunk4)
            ks = pl.multiple_of((j - 1) * chunk4, chunk4)
            o4, lse4 = _attend(gather(q16_ref, qs, chunk4), gather(k16_ref, ks, 2 * chunk4),
                               gather(v16_ref, ks, 2 * chunk4), bias4_full)
            d4_merge_store(r4, j, o4, lse4)
            return carry2

        lax.fori_loop(1, blocks4, d4_body, 0)
        return carry

    lax.fori_loop(0, 4, d4_class, 0)

    step = 512

    def out_body(t, carry):
        rows = pl.ds(pl.multiple_of(t * step, step), step)
        out_ref[rows, :] = (onat_ref[rows, :] * g_ref[rows, :].astype(F32)).astype(BF16)
        return carry

    lax.fori_loop(0, seq // step, out_body, 0)


def _attention_biases():
    W = N_KEYS
    ninf = -jnp.inf
    qi = jnp.arange(W)[:, None]
    kj = jnp.arange(2 * W)[None, :]
    dist = W + qi - kj
    full = jnp.where((dist >= 0) & (dist <= W), 0.0, ninf)
    first = jnp.where(kj <= qi, 0.0, ninf)
    cq = W // 4
    mq = 4 * (qi % cq) + qi // cq
    mk = 4 * (kj % (2 * cq) - cq) + kj // (2 * cq)
    d4 = mq - mk
    full4 = jnp.where((d4 >= 0) & (d4 <= W), 0.0, ninf)
    mk0 = 4 * (kj % cq) + kj // cq
    first4 = jnp.where(mk0 <= mq, 0.0, ninf)
    return jnp.stack([full, first, full4, first4]).astype(F32)


def _attention(h_nat, qkv16, batch, seq):
    bias = _attention_biases()
    n_sub16 = seq // MAX_DIL
    nat = lambda off: pl.BlockSpec((None, seq, LANES), lambda b, h: (b, 0, off + h))
    r16 = lambda off: pl.BlockSpec((None, MAX_DIL, n_sub16, LANES),
                                   lambda b, h: (b, 0, 0, off + h))
    gate_off = (3 * D_ATTN + D_POOL) // LANES
    kern = functools.partial(_attn_kernel, seq=seq)
    return pl.pallas_call(
        kern,
        grid=(batch, N_HEADS),
        in_specs=[nat(0), nat(N_HEADS), nat(2 * N_HEADS), nat(gate_off),
                  r16(0), r16(N_HEADS), r16(2 * N_HEADS),
                  pl.BlockSpec((4, N_KEYS, 2 * N_KEYS), lambda b, h: (0, 0, 0))],
        out_specs=pl.BlockSpec((None, seq, LANES), lambda b, h: (b, 0, h)),
        out_shape=jax.ShapeDtypeStruct((batch, seq, D_ATTN), BF16),
        scratch_shapes=[pltpu.VMEM((seq, LANES), F32) for _ in range(5)],
        compiler_params=pltpu.CompilerParams(
            dimension_semantics=("arbitrary", "arbitrary"),
            vmem_limit_bytes=VMEM_LIMIT),
        name="dilated_attention",
    )(h_nat, h_nat, h_nat, h_nat, qkv16, qkv16, qkv16, bias)


def _out_kernel(ya_ref, u_ref, uh_ref, gp_ref, wp_ref, ps_ref, wo_ref, x_ref, gain_ref,
                bias_ref, out_ref, pad_ref, y_ref, *, tm):
    i = pl.program_id(1)
    pad_ref[0:POOL_HALO, :] = jnp.where(i > 0, uh_ref[...].astype(F32), 0.0)
    pad_ref[POOL_HALO:, :] = u_ref[...].astype(F32)
    y_ref[:, 0:D_ATTN] = ya_ref[...]

    pos = i * tm + lax.broadcasted_iota(jnp.int32, (tm, 1), 0)
    for g, window in enumerate(POOL_WINDOWS):
        cols = slice(g * POOL_GROUP_DIM, (g + 1) * POOL_GROUP_DIM)
        tot = pad_ref[POOL_HALO:, cols]
        for back in range(1, window):
            tot = tot + pad_ref[POOL_HALO - back:POOL_HALO - back + tm, cols]
        count = jnp.minimum(pos + 1, window).astype(F32)
        pooled = tot / count - pad_ref[POOL_HALO:, cols]
        po = jnp.dot(pooled.astype(BF16), wp_ref[g], preferred_element_type=F32)
        po = po * ps_ref[:, cols] * gp_ref[:, cols].astype(F32)
        y_ref[:, D_ATTN + g * POOL_GROUP_DIM:D_ATTN + (g + 1) * POOL_GROUP_DIM] = po.astype(BF16)

    proj = jnp.dot(y_ref[...], wo_ref[...], preferred_element_type=F32)
    z = DEEPNORM_ALPHA * x_ref[...] + proj
    mu = jnp.mean(z, axis=-1, keepdims=True)
    zc = z - mu
    var = jnp.mean(zc * zc, axis=-1, keepdims=True)
    out_ref[...] = zc * lax.rsqrt(var + LN_EPS) * gain_ref[...] + bias_ref[...]


def _output(y_attn, h_nat, w_pool_b, pool_scale, w_out_b, x, gain, bias, batch, seq):
    tm = OUT_TM
    u_blk = 3 * D_ATTN // D_POOL
    gp_blk = (3 * D_ATTN + D_POOL + D_ATTN) // D_POOL
    halo_per_tile = tm // POOL_HALO
    const = lambda shape: pl.BlockSpec(shape, lambda b, i: (0,) * len(shape))
    kern = functools.partial(_out_kernel, tm=tm)
    return pl.pallas_call(
        kern,
        grid=(batch, seq // tm),
        in_specs=[
            pl.BlockSpec((None, tm, D_ATTN), lambda b, i: (b, i, 0)),
            pl.BlockSpec((None, tm, D_POOL), lambda b, i: (b, i, u_blk)),
            pl.BlockSpec((None, POOL_HALO, D_POOL),
                         lambda b, i: (b, jnp.maximum(i * halo_per_tile - 1, 0), u_blk)),
            pl.BlockSpec((None, tm, D_POOL), lambda b, i: (b, i, gp_blk)),
            const((len(POOL_WINDOWS), POOL_GROUP_DIM, POOL_GROUP_DIM)),
            const((1, D_POOL)),
            const((D_MODEL, D_MODEL)),
            pl.BlockSpec((None, tm, D_MODEL), lambda b, i: (b, i, 0)),
            const((1, D_MODEL)),
            const((1, D_MODEL)),
        ],
        out_specs=pl.BlockSpec((None, tm, D_MODEL), lambda b, i: (b, i, 0)),
        out_shape=jax.ShapeDtypeStruct((batch, seq, D_MODEL), F32),
        scratch_shapes=[
            pltpu.VMEM((tm + POOL_HALO, D_POOL), F32),
            pltpu.VMEM((tm, D_MODEL), BF16),
        ],
        compiler_params=pltpu.CompilerParams(
            dimension_semantics=("arbitrary", "arbitrary"),
            vmem_limit_bytes=VMEM_LIMIT),
        name="pool_out_deepnorm",
    )(y_attn, h_nat, h_nat, h_nat, w_pool_b, pool_scale, w_out_b, x, gain, bias)


def _rotary_tables(seq):
    inv_freq = ROPE_THETA ** (-(2.0 * jnp.arange(ROPE_HALF, dtype=F32)) / ROPE_DIM)
    ang = jnp.arange(seq, dtype=jnp.int32).astype(F32)[:, None] * inv_freq[None, :]
    cos, sin = jnp.cos(ang), jnp.sin(ang)
    zeros = jnp.zeros((seq, HEAD_DIM - ROPE_DIM), F32)
    zh = jnp.zeros((seq, ROPE_HALF), F32)
    c = jnp.concatenate([cos, cos, jnp.ones_like(zeros)], axis=1)
    a = jnp.concatenate([zh, sin, zeros], axis=1)
    b = jnp.concatenate([-sin, zh, zeros], axis=1)
    k_tab = jnp.stack([c, a, b])
    return jnp.stack([k_tab * (HEAD_DIM ** -0.5), k_tab])


def kernel(x, w_in, w_pool, pool_scale, w_out, ln_gain, ln_bias):
    batch, seq, d_model = x.shape
    assert d_model == D_MODEL and seq % (MAX_DIL * N_KEYS) == 0 and seq % PROJ_TM == 0
    assert w_in.shape[0] == 1, "single layer"
    rot = _rotary_tables(seq)
    h_nat, qkv16 = _projection(x.reshape(batch * seq, D_MODEL), w_in[0].astype(BF16), rot,
                               batch, seq)
    h_nat = h_nat.reshape(batch, seq, D_IN)
    y_attn = _attention(h_nat, qkv16, batch, seq)
    return _output(y_attn, h_nat, w_pool[0].astype(BF16), pool_scale.astype(F32),
                   w_out[0].astype(BF16), x, ln_gain.astype(F32), ln_bias.astype(F32),
                   batch, seq)
```

```python
import functools

import jax
import jax.numpy as jnp
from jax import lax
from jax.experimental import pallas as pl
from jax.experimental.pallas import tpu as pltpu

D_MODEL = 2048
D_ATTN = 1024
D_POOL = 1024
HEAD_DIM = 128
N_HEADS = D_ATTN // HEAD_DIM
ROPE_DIM = HEAD_DIM // 4
ROPE_HALF = ROPE_DIM // 2
ROPE_THETA = 500000.0
N_KEYS = 128
DILATIONS = (1, 4, 16)
MAX_DIL = 16
POOL_WINDOWS = (2, 4, 8, 16)
POOL_GROUP_DIM = D_POOL // len(POOL_WINDOWS)
POOL_HALO = 16
D_IN = 3 * D_ATTN + D_POOL + D_MODEL
LN_EPS = 1e-5
DEEPNORM_ALPHA = 2.0 ** 0.25

LANES = 128
VMEM_LIMIT = 56 * 1024 * 1024

PROJ_TM = 512
PROJ_TN = 512
OUT_TM = 512
ATTN_UNROLL = 8

BF16 = jnp.bfloat16
F32 = jnp.float32


def _proj_kernel(x_ref, w_ref, rot_ref, nat_ref, r16_ref, xb_ref, slab_ref, *, tm, tn):
    j = pl.program_id(1)
    heads = tn // LANES
    n_q = D_ATTN // tn
    n_qkv = 3 * n_q
    n_pool = D_POOL // tn

    @pl.when(j == 0)
    def _():
        xb_ref[...] = x_ref[...].astype(BF16)

    def matmul():
        return jnp.dot(xb_ref[...], w_ref[...], preferred_element_type=F32)

    def write_both(vals):
        for h, t in enumerate(vals):
            nat_ref[:, h * LANES:(h + 1) * LANES] = t.astype(BF16)
            slab_ref[h] = t
        rows = tm // MAX_DIL
        for r in range(MAX_DIL):
            for h in range(heads):
                t = slab_ref[h, pl.ds(r, rows, stride=MAX_DIL), :]
                r16_ref[r, :, h * LANES:(h + 1) * LANES] = t.astype(BF16)

    @pl.when(j < 2 * n_q)
    def _():
        res = matmul()
        c, a, b = rot_ref[0], rot_ref[1], rot_ref[2]
        vals = []
        for h in range(heads):
            t = res[:, h * LANES:(h + 1) * LANES]
            vals.append(t * c + pltpu.roll(t, ROPE_HALF, 1) * a
                        + pltpu.roll(t, LANES - ROPE_HALF, 1) * b)
        write_both(vals)

    @pl.when((j >= 2 * n_q) & (j < n_qkv))
    def _():
        res = matmul()
        write_both([res[:, h * LANES:(h + 1) * LANES] for h in range(heads)])

    @pl.when((j >= n_qkv) & (j < n_qkv + n_pool))
    def _():
        nat_ref[...] = matmul().astype(BF16)

    @pl.when(j >= n_qkv + n_pool)
    def _():
        g = matmul()
        nat_ref[...] = (g / (1.0 + jnp.exp(-g))).astype(BF16)


def _projection(x2, w_in_b, rot, batch, seq):
    tm, tn = PROJ_TM, PROJ_TN
    m = x2.shape[0]
    tiles_per_seq = seq // tm
    n_q = D_ATTN // tn
    last_qkv = 3 * n_q - 1
    grid = (m // tm, D_IN // tn)
    kern = functools.partial(_proj_kernel, tm=tm, tn=tn)
    return pl.pallas_call(
        kern,
        grid=grid,
        in_specs=[
            pl.BlockSpec((tm, D_MODEL), lambda i, j: (i, 0)),
            pl.BlockSpec((D_MODEL, tn), lambda i, j: (0, j)),
            pl.BlockSpec((None, 3, tm, LANES),
                         lambda i, j: (jnp.minimum(j // n_q, 1), 0, i % tiles_per_seq, 0)),
        ],
        out_specs=[
            pl.BlockSpec((tm, tn), lambda i, j: (i, j)),
            pl.BlockSpec((None, MAX_DIL, tm // MAX_DIL, tn),
                         lambda i, j: (i // tiles_per_seq, 0, i % tiles_per_seq,
                                       jnp.minimum(j, last_qkv))),
        ],
        out_shape=[
            jax.ShapeDtypeStruct((m, D_IN), BF16),
            jax.ShapeDtypeStruct((batch, MAX_DIL, seq // MAX_DIL, 3 * D_ATTN), BF16),
        ],
        scratch_shapes=[
            pltpu.VMEM((tm, D_MODEL), BF16),
            pltpu.VMEM((tn // LANES, tm, LANES), F32),
        ],
        compiler_params=pltpu.CompilerParams(
            dimension_semantics=("arbitrary", "arbitrary"),
            vmem_limit_bytes=VMEM_LIMIT),
        name="in_projection",
    )(x2, w_in_b, rot)


def _attend(q, k, vx, bias):
    s = lax.dot_general(q, k, (((1,), (1,)), ((), ())), preferred_element_type=F32) + bias
    m = jnp.max(s, axis=1, keepdims=True)
    p = jnp.exp(s - m)
    acc = jnp.dot(p.astype(BF16), vx, preferred_element_type=F32)
    den = acc[:, HEAD_DIM:]
    return acc[:, :HEAD_DIM] / den, m + jnp.log(den)


def _attn_kernel(q_ref, k_ref, v_ref, g_ref, q16_ref, k16_ref, v16_ref, bias_ref, out_ref,
                 o1_ref, l1_ref, o16_ref, l16_ref, onat_ref, vx_ref, vx16_ref, *, seq):
    W = N_KEYS
    n_sub16 = seq // MAX_DIL
    blocks = seq // W
    blocks16 = n_sub16 // W
    blocks4 = seq // 4 // W
    chunk4 = W // 4
    step = 512

    def fill_vx(t, carry):
        rows = pl.ds(pl.multiple_of(t * step, step), step)
        ones = jnp.ones((step, HEAD_DIM), BF16)
        vx_ref[rows, :HEAD_DIM] = v_ref[rows, :]
        vx_ref[rows, HEAD_DIM:] = ones
        vx16_ref[rows, :HEAD_DIM] = v16_ref[rows, :]
        vx16_ref[rows, HEAD_DIM:] = ones
        return carry

    lax.fori_loop(0, seq // step, fill_vx, 0)

    def banded_pass(qr, kr, vr, o_ref, l_ref, blocks_per_seq):
        def body(it, carry):
            for u in range(ATTN_UNROLL):
                g = it * ATTN_UNROLL + u
                kstart = pl.multiple_of(jnp.maximum(g - 1, 0) * W, W)
                first = (g % blocks_per_seq) == 0
                bidx = jnp.where(g == 0, 2, jnp.where(first, 1, 0))
                qrows = pl.ds(pl.multiple_of(g * W, W), W)
                krows = pl.ds(kstart, 2 * W)
                o, lse = _attend(qr[qrows, :], kr[krows, :], vr[krows, :], bias_ref[bidx])
                o_ref[qrows, :] = o
                l_ref[qrows, :] = lse
            return carry

        lax.fori_loop(0, blocks // ATTN_UNROLL, body, 0)

    banded_pass(q_ref, k_ref, vx_ref, o1_ref, l1_ref, blocks)
    banded_pass(q16_ref, k16_ref, vx16_ref, o16_ref, l16_ref, blocks16)

    def d4_merge_store(r4, j, o4, lse4):
        for c in range(4):
            cls = 4 * c + r4
            rows16 = pl.ds(pl.multiple_of(cls * n_sub16 + j * chunk4, chunk4), chunk4)
            rows_nat = pl.ds(j * chunk4 * MAX_DIL + cls, chunk4, stride=MAX_DIL)
            oc4 = o4[c * chunk4:(c + 1) * chunk4, :]
            lc4 = lse4[c * chunk4:(c + 1) * chunk4, :]
            oc16, lc16 = o16_ref[rows16, :], l16_ref[rows16, :]
            oc1, lc1 = o1_ref[rows_nat, :], l1_ref[rows_nat, :]
            mx = jnp.maximum(jnp.maximum(lc1, lc4), lc16)
            w1, w4, w16 = jnp.exp(lc1 - mx), jnp.exp(lc4 - mx), jnp.exp(lc16 - mx)
            mixed = (w1 * oc1 + w4 * oc4 + w16 * oc16) / (w1 + w4 + w16)
            onat_ref[rows_nat, :] = mixed

    def d4_body(it, carry):
        for u in range(ATTN_UNROLL):
            g = it * ATTN_UNROLL + u
            r4 = g // blocks4
            j = g % blocks4
            qs = pl.multiple_of(j * chunk4, chunk4)
            ks = pl.multiple_of(jnp.maximum(j - 1, 0) * chunk4, chunk4)

            def gather(ref, start, size):
                return jnp.concatenate(
                    [ref[pl.ds(pl.multiple_of((4 * c + r4) * n_sub16 + start, chunk4), size), :]
                     for c in range(4)], axis=0)

            bidx = jnp.where(j == 0, 4, 3)
            o4, lse4 = _attend(gather(q16_ref, qs, chunk4), gather(k16_ref, ks, 2 * chunk4),
                               gather(vx16_ref, ks, 2 * chunk4), bias_ref[bidx])
            d4_merge_store(r4, j, o4, lse4)
        return carry

    lax.fori_loop(0, blocks // ATTN_UNROLL, d4_body, 0)

    def out_body(t, carry):
        rows = pl.ds(pl.multiple_of(t * step, step), step)
        out_ref[rows, :] = (onat_ref[rows, :] * g_ref[rows, :].astype(F32)).astype(BF16)
        return carry

    lax.fori_loop(0, seq // step, out_body, 0)


def _attention_biases():
    W = N_KEYS
    ninf = -jnp.inf
    qi = jnp.arange(W)[:, None]
    kj = jnp.arange(2 * W)[None, :]
    dist = W + qi - kj
    full = (dist >= 0) & (dist <= W)
    first = (dist >= 0) & (kj >= W)
    first0 = kj <= qi
    cq = W // 4
    mq = 4 * (qi % cq) + qi // cq
    mk = 4 * (kj % (2 * cq) - cq) + kj // (2 * cq)
    full4 = (mq - mk >= 0) & (mq - mk <= W)
    mk0 = 4 * (kj % (2 * cq)) + kj // (2 * cq)
    first4 = mk0 <= mq
    masks = jnp.stack([full, first, first0, full4, first4])
    return jnp.where(masks, 0.0, ninf).astype(F32)


def _attention(h_nat, qkv16, batch, seq):
    bias = _attention_biases()
    blk = lambda off: pl.BlockSpec((None, seq, LANES), lambda b, h: (b, 0, off + h))
    gate_off = (3 * D_ATTN + D_POOL) // LANES
    kern = functools.partial(_attn_kernel, seq=seq)
    return pl.pallas_call(
        kern,
        grid=(batch, N_HEADS),
        in_specs=[blk(0), blk(N_HEADS), blk(2 * N_HEADS), blk(gate_off),
                  blk(0), blk(N_HEADS), blk(2 * N_HEADS),
                  pl.BlockSpec((5, N_KEYS, 2 * N_KEYS), lambda b, h: (0, 0, 0))],
        out_specs=pl.BlockSpec((None, seq, LANES), lambda b, h: (b, 0, h)),
        out_shape=jax.ShapeDtypeStruct((batch, seq, D_ATTN), BF16),
        scratch_shapes=[pltpu.VMEM((seq, LANES), F32) for _ in range(5)]
        + [pltpu.VMEM((seq, 2 * HEAD_DIM), BF16) for _ in range(2)],
        compiler_params=pltpu.CompilerParams(
            dimension_semantics=("arbitrary", "arbitrary"),
            vmem_limit_bytes=VMEM_LIMIT),
        name="dilated_attention",
    )(h_nat, h_nat, h_nat, h_nat, qkv16, qkv16, qkv16, bias)


def _out_kernel(ya_ref, u_ref, uh_ref, gp_ref, wp_ref, ps_ref, wo_ref, x_ref, gain_ref,
                bias_ref, out_ref, pad_ref, y_ref, *, tm):
    i = pl.program_id(1)
    pad_ref[0:POOL_HALO, :] = jnp.where(i > 0, uh_ref[...].astype(F32), 0.0)
    pad_ref[POOL_HALO:, :] = u_ref[...].astype(F32)
    y_ref[:, 0:D_ATTN] = ya_ref[...]

    pos = i * tm + lax.broadcasted_iota(jnp.int32, (tm, 1), 0)
    for g, window in enumerate(POOL_WINDOWS):
        cols = slice(g * POOL_GROUP_DIM, (g + 1) * POOL_GROUP_DIM)
        tot = pad_ref[POOL_HALO:, cols]
        for back in range(1, window):
            tot = tot + pad_ref[POOL_HALO - back:POOL_HALO - back + tm, cols]
        count = jnp.minimum(pos + 1, window).astype(F32)
        pooled = tot / count - pad_ref[POOL_HALO:, cols]
        po = jnp.dot(pooled.astype(BF16), wp_ref[g], preferred_element_type=F32)
        po = po * ps_ref[:, cols] * gp_ref[:, cols].astype(F32)
        y_ref[:, D_ATTN + g * POOL_GROUP_DIM:D_ATTN + (g + 1) * POOL_GROUP_DIM] = po.astype(BF16)

    proj = jnp.dot(y_ref[...], wo_ref[...], preferred_element_type=F32)
    z = DEEPNORM_ALPHA * x_ref[...] + proj
    mu = jnp.mean(z, axis=-1, keepdims=True)
    zc = z - mu
    var = jnp.mean(zc * zc, axis=-1, keepdims=True)
    out_ref[...] = zc * lax.rsqrt(var + LN_EPS) * gain_ref[...] + bias_ref[...]


def _output(y_attn, h_nat, w_pool_b, pool_scale, w_out_b, x, gain, bias, batch, seq):
    tm = OUT_TM
    u_blk = 3 * D_ATTN // D_POOL
    gp_blk = (3 * D_ATTN + D_POOL + D_ATTN) // D_POOL
    halo_per_tile = tm // POOL_HALO
    const = lambda shape: pl.BlockSpec(shape, lambda b, i: (0,) * len(shape))
    kern = functools.partial(_out_kernel, tm=tm)
    return pl.pallas_call(
        kern,
        grid=(batch, seq // tm),
        in_specs=[
            pl.BlockSpec((None, tm, D_ATTN), lambda b, i: (b, i, 0)),
            pl.BlockSpec((None, tm, D_POOL), lambda b, i: (b, i, u_blk)),
            pl.BlockSpec((None, POOL_HALO, D_POOL),
                         lambda b, i: (b, jnp.maximum(i * halo_per_tile - 1, 0), u_blk)),
            pl.BlockSpec((None, tm, D_POOL), lambda b, i: (b, i, gp_blk)),
            const((len(POOL_WINDOWS), POOL_GROUP_DIM, POOL_GROUP_DIM)),
            const((1, D_POOL)),
            const((D_MODEL, D_MODEL)),
            pl.BlockSpec((None, tm, D_MODEL), lambda b, i: (b, i, 0)),
            const((1, D_MODEL)),
            const((1, D_MODEL)),
        ],
        out_specs=pl.BlockSpec((None, tm, D_MODEL), lambda b, i: (b, i, 0)),
        out_shape=jax.ShapeDtypeStruct((batch, seq, D_MODEL), F32),
        scratch_shapes=[
            pltpu.VMEM((tm + POOL_HALO, D_POOL), F32),
            pltpu.VMEM((tm, D_MODEL), BF16),
        ],
        compiler_params=pltpu.CompilerParams(
            dimension_semantics=("arbitrary", "arbitrary"),
            vmem_limit_bytes=VMEM_LIMIT),
        name="pool_out_deepnorm",
    )(y_attn, h_nat, h_nat, h_nat, w_pool_b, pool_scale, w_out_b, x, gain, bias)


def _rotary_tables(seq):
    inv_freq = ROPE_THETA ** (-(2.0 * jnp.arange(ROPE_HALF, dtype=F32)) / ROPE_DIM)
    ang = jnp.arange(seq, dtype=jnp.int32).astype(F32)[:, None] * inv_freq[None, :]
    cos, sin = jnp.cos(ang), jnp.sin(ang)
    zeros = jnp.zeros((seq, HEAD_DIM - ROPE_DIM), F32)
    zh = jnp.zeros((seq, ROPE_HALF), F32)
    c = jnp.concatenate([cos, cos, jnp.ones_like(zeros)], axis=1)
    a = jnp.concatenate([zh, sin, zeros], axis=1)
    b = jnp.concatenate([-sin, zh, zeros], axis=1)
    k_tab = jnp.stack([c, a, b])
    return jnp.stack([k_tab * (HEAD_DIM ** -0.5), k_tab])


def kernel(x, w_in, w_pool, pool_scale, w_out, ln_gain, ln_bias):
    batch, seq, d_model = x.shape
    assert d_model == D_MODEL and seq % (MAX_DIL * N_KEYS) == 0 and seq % PROJ_TM == 0
    assert w_in.shape[0] == 1, "single layer"
    rot = _rotary_tables(seq)
    h_nat, qkv16 = _projection(x.reshape(batch * seq, D_MODEL), w_in[0].astype(BF16), rot,
                               batch, seq)
    h_nat = h_nat.reshape(batch, seq, D_IN)
    y_attn = _attention(h_nat, qkv16.reshape(batch, seq, 3 * D_ATTN), batch, seq)
    return _output(y_attn, h_nat, w_pool[0].astype(BF16), pool_scale.astype(F32),
                   w_out[0].astype(BF16), x, ln_gain.astype(F32), ln_bias.astype(F32),
                   batch, seq)
```

```python
import functools

import jax
import jax.numpy as jnp
from jax import lax
from jax.experimental import pallas as pl
from jax.experimental.pallas import tpu as pltpu

D_MODEL = 2048
D_ATTN = 1024
D_POOL = 1024
HEAD_DIM = 128
N_HEADS = D_ATTN // HEAD_DIM
ROPE_DIM = HEAD_DIM // 4
ROPE_HALF = ROPE_DIM // 2
ROPE_THETA = 500000.0
N_KEYS = 128
DILATIONS = (1, 4, 16)
MAX_DIL = 16
POOL_WINDOWS = (2, 4, 8, 16)
POOL_GROUP_DIM = D_POOL // len(POOL_WINDOWS)
POOL_HALO = 16
D_IN = 3 * D_ATTN + D_POOL + D_MODEL
LN_EPS = 1e-5
DEEPNORM_ALPHA = 2.0 ** 0.25

LANES = 128
VMEM_LIMIT = 56 * 1024 * 1024

PROJ_TM = 1024
PROJ_TN = 512
PROJ_CHUNK = 256
OUT_TM = 512
ATTN_UNROLL = 8

BF16 = jnp.bfloat16
F32 = jnp.float32


def _proj_kernel(x_ref, w_ref, rot_ref, nat_ref, r16_ref, xb_ref, slab_ref, *, tm, tn, cm):
    j = pl.program_id(1)
    heads = tn // LANES
    n_q = D_ATTN // tn
    n_qkv = 3 * n_q
    n_pool = D_POOL // tn

    @pl.when(j == 0)
    def _():
        xb_ref[...] = x_ref[...].astype(BF16)

    n_chunks = tm // cm
    sub = cm // MAX_DIL

    def matmul(c):
        return jnp.dot(xb_ref[c * cm:(c + 1) * cm, :], w_ref[...], preferred_element_type=F32)

    def to_slabs(c):
        res = matmul(c)
        for h in range(heads):
            slab_ref[c, h] = res[:, h * LANES:(h + 1) * LANES]

    def write_both(c, rotary):
        rows = slice(c * cm, (c + 1) * cm)
        for h in range(heads):
            t = slab_ref[c, h]
            if rotary:
                t = (t * rot_ref[0, rows, :] + pltpu.roll(t, ROPE_HALF, 1) * rot_ref[1, rows, :]
                     + pltpu.roll(t, LANES - ROPE_HALF, 1) * rot_ref[2, rows, :])
                slab_ref[c, h] = t
            nat_ref[rows, h * LANES:(h + 1) * LANES] = t.astype(BF16)
        for r in range(MAX_DIL):
            for h in range(heads):
                t = slab_ref[c, h, pl.ds(r, sub, stride=MAX_DIL), :]
                r16_ref[r, c * sub:(c + 1) * sub, h * LANES:(h + 1) * LANES] = t.astype(BF16)

    @pl.when(j < 2 * n_q)
    def _():
        for c in range(n_chunks):
            to_slabs(c)
            write_both(c, True)

    @pl.when((j >= 2 * n_q) & (j < n_qkv))
    def _():
        for c in range(n_chunks):
            to_slabs(c)
            write_both(c, False)

    @pl.when((j >= n_qkv) & (j < n_qkv + n_pool))
    def _():
        for c in range(n_chunks):
            nat_ref[c * cm:(c + 1) * cm, :] = matmul(c).astype(BF16)

    @pl.when(j >= n_qkv + n_pool)
    def _():
        for c in range(n_chunks):
            g = matmul(c)
            nat_ref[c * cm:(c + 1) * cm, :] = (g / (1.0 + jnp.exp(-g))).astype(BF16)


def _projection(x2, w_in_b, rot, batch, seq):
    tm, tn, cm = PROJ_TM, PROJ_TN, PROJ_CHUNK
    m = x2.shape[0]
    tiles_per_seq = seq // tm
    n_q = D_ATTN // tn
    last_qkv = 3 * n_q - 1
    grid = (m // tm, D_IN // tn)
    kern = functools.partial(_proj_kernel, tm=tm, tn=tn, cm=cm)
    return pl.pallas_call(
        kern,
        grid=grid,
        in_specs=[
            pl.BlockSpec((tm, D_MODEL), lambda i, j: (i, 0)),
            pl.BlockSpec((D_MODEL, tn), lambda i, j: (0, j)),
            pl.BlockSpec((None, 3, tm, LANES),
                         lambda i, j: (jnp.minimum(j // n_q, 1), 0, i % tiles_per_seq, 0)),
        ],
        out_specs=[
            pl.BlockSpec((tm, tn), lambda i, j: (i, j)),
            pl.BlockSpec((None, MAX_DIL, tm // MAX_DIL, tn),
                         lambda i, j: (i // tiles_per_seq, 0, i % tiles_per_seq,
                                       jnp.minimum(j, last_qkv))),
        ],
        out_shape=[
            jax.ShapeDtypeStruct((m, D_IN), BF16),
            jax.ShapeDtypeStruct((batch, MAX_DIL, seq // MAX_DIL, 3 * D_ATTN), BF16),
        ],
        scratch_shapes=[
            pltpu.VMEM((tm, D_MODEL), BF16),
            pltpu.VMEM((tm // cm, tn // LANES, cm, LANES), F32),
        ],
        compiler_params=pltpu.CompilerParams(
            dimension_semantics=("arbitrary", "arbitrary"),
            vmem_limit_bytes=VMEM_LIMIT),
        name="in_projection",
    )(x2, w_in_b, rot)


def _attend(q, k, vx, bias):
    s = lax.dot_general(q, k, (((1,), (1,)), ((), ())), preferred_element_type=F32) + bias
    m = jnp.max(s, axis=1, keepdims=True)
    p = jnp.exp(s - m)
    acc = jnp.dot(p.astype(BF16), vx, preferred_element_type=F32)
    den = acc[:, HEAD_DIM:]
    return acc[:, :HEAD_DIM] / den, m + jnp.log(den)


def _attn_kernel(q_ref, k_ref, v_ref, g_ref, q16_ref, k16_ref, v16_ref, bias_ref, out_ref,
                 o1_ref, l1_ref, o16_ref, l16_ref, onat_ref, vx_ref, vx16_ref, *, seq):
    W = N_KEYS
    n_sub16 = seq // MAX_DIL
    blocks = seq // W
    blocks16 = n_sub16 // W
    blocks4 = seq // 4 // W
    chunk4 = W // 4
    step = 512

    def fill_vx(t, carry):
        rows = pl.ds(pl.multiple_of(t * step, step), step)
        ones = jnp.ones((step, HEAD_DIM), BF16)
        vx_ref[rows, :HEAD_DIM] = v_ref[rows, :]
        vx_ref[rows, HEAD_DIM:] = ones
        vx16_ref[rows, :HEAD_DIM] = v16_ref[rows, :]
        vx16_ref[rows, HEAD_DIM:] = ones
        return carry

    lax.fori_loop(0, seq // step, fill_vx, 0)

    def banded_pass(qr, kr, vr, o_ref, l_ref, blocks_per_seq):
        def body(it, carry):
            for u in range(ATTN_UNROLL):
                g = it * ATTN_UNROLL + u
                kstart = pl.multiple_of(jnp.maximum(g - 1, 0) * W, W)
                first = (g % blocks_per_seq) == 0
                bidx = jnp.where(g == 0, 2, jnp.where(first, 1, 0))
                qrows = pl.ds(pl.multiple_of(g * W, W), W)
                krows = pl.ds(kstart, 2 * W)
                o, lse = _attend(qr[qrows, :], kr[krows, :], vr[krows, :], bias_ref[bidx])
                o_ref[qrows, :] = o
                l_ref[qrows, :] = lse
            return carry

        lax.fori_loop(0, blocks // ATTN_UNROLL, body, 0)

    banded_pass(q_ref, k_ref, vx_ref, o1_ref, l1_ref, blocks)
    banded_pass(q16_ref, k16_ref, vx16_ref, o16_ref, l16_ref, blocks16)

    def d4_merge_store(r4, j, o4, lse4):
        for c in range(4):
            cls = 4 * c + r4
            rows16 = pl.ds(pl.multiple_of(cls * n_sub16 + j * chunk4, chunk4), chunk4)
            rows_nat = pl.ds(j * chunk4 * MAX_DIL + cls, chunk4, stride=MAX_DIL)
            oc4 = o4[c * chunk4:(c + 1) * chunk4, :]
            lc4 = lse4[c * chunk4:(c + 1) * chunk4, :]
            oc16, lc16 = o16_ref[rows16, :], l16_ref[rows16, :]
            oc1, lc1 = o1_ref[rows_nat, :], l1_ref[rows_nat, :]
            mx = jnp.maximum(jnp.maximum(lc1, lc4), lc16)
            w1, w4, w16 = jnp.exp(lc1 - mx), jnp.exp(lc4 - mx), jnp.exp(lc16 - mx)
            mixed = (w1 * oc1 + w4 * oc4 + w16 * oc16) / (w1 + w4 + w16)
            onat_ref[rows_nat, :] = mixed

    def d4_body(it, carry):
        for u in range(ATTN_UNROLL):
            g = it * ATTN_UNROLL + u
            r4 = g // blocks4
            j = g % blocks4
            qs = pl.multiple_of(j * chunk4, chunk4)
            ks = pl.multiple_of(jnp.maximum(j - 1, 0) * chunk4, chunk4)

            def gather(ref, start, size):
                return jnp.concatenate(
                    [ref[pl.ds(pl.multiple_of((4 * c + r4) * n_sub16 + start, chunk4), size), :]
                     for c in range(4)], axis=0)

            bidx = jnp.where(j == 0, 4, 3)
            o4, lse4 = _attend(gather(q16_ref, qs, chunk4), gather(k16_ref, ks, 2 * chunk4),
                               gather(vx16_ref, ks, 2 * chunk4), bias_ref[bidx])
            d4_merge_store(r4, j, o4, lse4)
        return carry

    lax.fori_loop(0, blocks // ATTN_UNROLL, d4_body, 0)

    def out_body(t, carry):
        rows = pl.ds(pl.multiple_of(t * step, step), step)
        out_ref[rows, :] = (onat_ref[rows, :] * g_ref[rows, :].astype(F32)).astype(BF16)
        return carry

    lax.fori_loop(0, seq // step, out_body, 0)


def _attention_biases():
    W = N_KEYS
    ninf = -jnp.inf
    qi = jnp.arange(W)[:, None]
    kj = jnp.arange(2 * W)[None, :]
    dist = W + qi - kj
    full = (dist >= 0) & (dist <= W)
    first = (dist >= 0) & (kj >= W)
    first0 = kj <= qi
    cq = W // 4
    mq = 4 * (qi % cq) + qi // cq
    mk = 4 * (kj % (2 * cq) - cq) + kj // (2 * cq)
    full4 = (mq - mk >= 0) & (mq - mk <= W)
    mk0 = 4 * (kj % (2 * cq)) + kj // (2 * cq)
    first4 = mk0 <= mq
    masks = jnp.stack([full, first, first0, full4, first4])
    return jnp.where(masks, 0.0, ninf).astype(F32)


def _attention(h_nat, qkv16, batch, seq):
    bias = _attention_biases()
    blk = lambda off: pl.BlockSpec((None, seq, LANES), lambda b, h: (b, 0, off + h))
    gate_off = (3 * D_ATTN + D_POOL) // LANES
    kern = functools.partial(_attn_kernel, seq=seq)
    return pl.pallas_call(
        kern,
        grid=(batch, N_HEADS),
        in_specs=[blk(0), blk(N_HEADS), blk(2 * N_HEADS), blk(gate_off),
                  blk(0), blk(N_HEADS), blk(2 * N_HEADS),
                  pl.BlockSpec((5, N_KEYS, 2 * N_KEYS), lambda b, h: (0, 0, 0))],
        out_specs=pl.BlockSpec((None, seq, LANES), lambda b, h: (b, 0, h)),
        out_shape=jax.ShapeDtypeStruct((batch, seq, D_ATTN), BF16),
        scratch_shapes=[pltpu.VMEM((seq, LANES), F32) for _ in range(5)]
        + [pltpu.VMEM((seq, 2 * HEAD_DIM), BF16) for _ in range(2)],
        compiler_params=pltpu.CompilerParams(
            dimension_semantics=("arbitrary", "arbitrary"),
            vmem_limit_bytes=VMEM_LIMIT),
        name="dilated_attention",
    )(h_nat, h_nat, h_nat, h_nat, qkv16, qkv16, qkv16, bias)


def _out_kernel(ya_ref, u_ref, uh_ref, gp_ref, wp_ref, ps_ref, wo_ref, x_ref, gain_ref,
                bias_ref, out_ref, pad_ref, y_ref, *, tm):
    i = pl.program_id(1)
    pad_ref[0:POOL_HALO, :] = jnp.where(i > 0, uh_ref[...].astype(F32), 0.0)
    pad_ref[POOL_HALO:, :] = u_ref[...].astype(F32)
    y_ref[:, 0:D_ATTN] = ya_ref[...]

    pos = i * tm + lax.broadcasted_iota(jnp.int32, (tm, 1), 0)
    for g, window in enumerate(POOL_WINDOWS):
        cols = slice(g * POOL_GROUP_DIM, (g + 1) * POOL_GROUP_DIM)
        tot = pad_ref[POOL_HALO:, cols]
        for back in range(1, window):
            tot = tot + pad_ref[POOL_HALO - back:POOL_HALO - back + tm, cols]
        count = jnp.minimum(pos + 1, window).astype(F32)
        pooled = tot / count - pad_ref[POOL_HALO:, cols]
        po = jnp.dot(pooled.astype(BF16), wp_ref[g], preferred_element_type=F32)
        po = po * ps_ref[:, cols] * gp_ref[:, cols].astype(F32)
        y_ref[:, D_ATTN + g * POOL_GROUP_DIM:D_ATTN + (g + 1) * POOL_GROUP_DIM] = po.astype(BF16)

    proj = jnp.dot(y_ref[...], wo_ref[...], preferred_element_type=F32)
    z = DEEPNORM_ALPHA * x_ref[...] + proj
    mu = jnp.mean(z, axis=-1, keepdims=True)
    zc = z - mu
    var = jnp.mean(zc * zc, axis=-1, keepdims=True)
    out_ref[...] = zc * lax.rsqrt(var + LN_EPS) * gain_ref[...] + bias_ref[...]


def _output(y_attn, h_nat, w_pool_b, pool_scale, w_out_b, x, gain, bias, batch, seq):
    tm = OUT_TM
    u_blk = 3 * D_ATTN // D_POOL
    gp_blk = (3 * D_ATTN + D_POOL + D_ATTN) // D_POOL
    halo_per_tile = tm // POOL_HALO
    const = lambda shape: pl.BlockSpec(shape, lambda b, i: (0,) * len(shape))
    kern = functools.partial(_out_kernel, tm=tm)
    return pl.pallas_call(
        kern,
        grid=(batch, seq // tm),
        in_specs=[
            pl.BlockSpec((None, tm, D_ATTN), lambda b, i: (b, i, 0)),
            pl.BlockSpec((None, tm, D_POOL), lambda b, i: (b, i, u_blk)),
            pl.BlockSpec((None, POOL_HALO, D_POOL),
                         lambda b, i: (b, jnp.maximum(i * halo_per_tile - 1, 0), u_blk)),
            pl.BlockSpec((None, tm, D_POOL), lambda b, i: (b, i, gp_blk)),
            const((len(POOL_WINDOWS), POOL_GROUP_DIM, POOL_GROUP_DIM)),
            const((1, D_POOL)),
            const((D_MODEL, D_MODEL)),
            pl.BlockSpec((None, tm, D_MODEL), lambda b, i: (b, i, 0)),
            const((1, D_MODEL)),
            const((1, D_MODEL)),
        ],
        out_specs=pl.BlockSpec((None, tm, D_MODEL), lambda b, i: (b, i, 0)),
        out_shape=jax.ShapeDtypeStruct((batch, seq, D_MODEL), F32),
        scratch_shapes=[
            pltpu.VMEM((tm + POOL_HALO, D_POOL), F32),
            pltpu.VMEM((tm, D_MODEL), BF16),
        ],
        compiler_params=pltpu.CompilerParams(
            dimension_semantics=("arbitrary", "arbitrary"),
            vmem_limit_bytes=VMEM_LIMIT),
        name="pool_out_deepnorm",
    )(y_attn, h_nat, h_nat, h_nat, w_pool_b, pool_scale, w_out_b, x, gain, bias)


def _rotary_tables(seq):
    inv_freq = ROPE_THETA ** (-(2.0 * jnp.arange(ROPE_HALF, dtype=F32)) / ROPE_DIM)
    ang = jnp.arange(seq, dtype=jnp.int32).astype(F32)[:, None] * inv_freq[None, :]
    cos, sin = jnp.cos(ang), jnp.sin(ang)
    zeros = jnp.zeros((seq, HEAD_DIM - ROPE_DIM), F32)
    zh = jnp.zeros((seq, ROPE_HALF), F32)
    c = jnp.concatenate([cos, cos, jnp.ones_like(zeros)], axis=1)
    a = jnp.concatenate([zh, sin, zeros], axis=1)
    b = jnp.concatenate([-sin, zh, zeros], axis=1)
    k_tab = jnp.stack([c, a, b])
    return jnp.stack([k_tab * (HEAD_DIM ** -0.5), k_tab])


def kernel(x, w_in, w_pool, pool_scale, w_out, ln_gain, ln_bias):
    batch, seq, d_model = x.shape
    assert d_model == D_MODEL and seq % (MAX_DIL * N_KEYS) == 0 and seq % PROJ_TM == 0
    assert w_in.shape[0] == 1, "single layer"
    rot = _rotary_tables(seq)
    h_nat, qkv16 = _projection(x.reshape(batch * seq, D_MODEL), w_in[0].astype(BF16), rot,
                               batch, seq)
    h_nat = h_nat.reshape(batch, seq, D_IN)
    y_attn = _attention(h_nat, qkv16.reshape(batch, seq, 3 * D_ATTN), batch, seq)
    return _output(y_attn, h_nat, w_pool[0].astype(BF16), pool_scale.astype(F32),
                   w_out[0].astype(BF16), x, ln_gain.astype(F32), ln_bias.astype(F32),
                   batch, seq)
```

```python
import functools

import jax
import jax.numpy as jnp
from jax import lax
from jax.experimental import pallas as pl
from jax.experimental.pallas import tpu as pltpu

D_MODEL = 2048
D_ATTN = 1024
D_POOL = 1024
HEAD_DIM = 128
N_HEADS = D_ATTN // HEAD_DIM
ROPE_DIM = HEAD_DIM // 4
ROPE_HALF = ROPE_DIM // 2
ROPE_THETA = 500000.0
N_KEYS = 128
DILATIONS = (1, 4, 16)
MAX_DIL = 16
POOL_WINDOWS = (2, 4, 8, 16)
POOL_GROUP_DIM = D_POOL // len(POOL_WINDOWS)
POOL_HALO = 16
D_IN = 3 * D_ATTN + D_POOL + D_MODEL
LN_EPS = 1e-5
DEEPNORM_ALPHA = 2.0 ** 0.25

LANES = 128
VMEM_LIMIT = 56 * 1024 * 1024

PROJ_TM = 1024
PROJ_TN = 512
OUT_TM = 512
ATTN_UNROLL = 8

BF16 = jnp.bfloat16
F32 = jnp.float32


def _proj_kernel(x_ref, w_ref, rot_ref, nat_ref, r16_ref, xb_ref, raw0_ref, raw1_ref, tmp_ref, *,
                 tm, tn, n_tiles):
    s = pl.program_id(0)
    heads = tn // LANES
    n_col = D_IN // tn
    n_q = D_ATTN // tn
    n_qkv = 3 * n_q
    n_gate0 = n_qkv + D_POOL // tn
    sub = tm // MAX_DIL
    col_e = jnp.maximum(s - 1, 0) % n_col

    @pl.when((s % n_col == 0) & (s < n_tiles))
    def _():
        xb_ref[...] = x_ref[...].astype(BF16)

    raw = (raw0_ref, raw1_ref)

    def matmul_to(raw_ref):
        res = jnp.dot(xb_ref[...], w_ref[...], preferred_element_type=F32)
        for h in range(heads):
            raw_ref[h] = res[:, h * LANES:(h + 1) * LANES]

    def qkv_epilogue(raw_ref):
        is_v = col_e >= 2 * n_q
        rc = 128
        for c0 in range(0, tm, rc):
            rows = slice(c0, c0 + rc)
            ct, at, bt = rot_ref[0, rows, :], rot_ref[1, rows, :], rot_ref[2, rows, :]
            for h in range(heads):
                t = raw_ref[h, rows, :]
                rot = (t * ct + pltpu.roll(t, ROPE_HALF, 1) * at
                       + pltpu.roll(t, LANES - ROPE_HALF, 1) * bt)
                t = jnp.where(is_v, t, rot)
                raw_ref[h, rows, :] = t
                nat_ref[rows, h * LANES:(h + 1) * LANES] = t.astype(BF16)
        quarter = tm // 4
        for h in range(heads):
            for r4 in range(4):
                tmp_ref[h, r4 * quarter:(r4 + 1) * quarter, :] = (
                    raw_ref[h, pl.ds(r4, quarter, stride=4), :])
        for r4 in range(4):
            for c in range(4):
                for h in range(heads):
                    t = tmp_ref[h, pl.ds(r4 * quarter + c, sub, stride=4), :]
                    r16_ref[4 * c + r4, :, h * LANES:(h + 1) * LANES] = t.astype(BF16)

    def rest_epilogue(raw_ref):
        is_gate = col_e >= n_gate0
        for h in range(heads):
            g = raw_ref[h]
            silu = g / (1.0 + jnp.exp(-g))
            nat_ref[:, h * LANES:(h + 1) * LANES] = jnp.where(is_gate, silu, g).astype(BF16)

    interior = (s >= 1) & (s < n_tiles)
    for parity in (0, 1):
        here = interior & (s % 2 == parity)

        @pl.when(here & (col_e < n_qkv))
        def _():
            qkv_epilogue(raw[1 - parity])
            matmul_to(raw[parity])

        @pl.when(here & (col_e >= n_qkv))
        def _():
            rest_epilogue(raw[1 - parity])
            matmul_to(raw[parity])

    @pl.when(s == 0)
    def _():
        matmul_to(raw[0])

    @pl.when(s == n_tiles)
    def _():
        rest_epilogue(raw[(n_tiles - 1) % 2])


def _projection(x2, w_in_b, rot, batch, seq):
    tm, tn = PROJ_TM, PROJ_TN
    m = x2.shape[0]
    tiles_per_seq = seq // tm
    n_col = D_IN // tn
    n_q = D_ATTN // tn
    last_qkv = 3 * n_q - 1
    n_tiles = (m // tm) * n_col
    assert n_col % 2 == 0
    kern = functools.partial(_proj_kernel, tm=tm, tn=tn, n_tiles=n_tiles)

    def mm_tile(s):
        t = jnp.minimum(s, n_tiles - 1)
        return t // n_col, t % n_col

    def ep_tile(s):
        t = jnp.maximum(s - 1, 0)
        return t // n_col, t % n_col

    def rot_map(s):
        i, j = ep_tile(s)
        return (jnp.minimum(j // n_q, 1), 0, i % tiles_per_seq, 0)

    def r16_map(s):
        i, j = ep_tile(s)
        return (i // tiles_per_seq, 0, i % tiles_per_seq, jnp.minimum(j, last_qkv))

    return pl.pallas_call(
        kern,
        grid=(n_tiles + 1,),
        in_specs=[
            pl.BlockSpec((tm, D_MODEL), lambda s: (mm_tile(s)[0], 0)),
            pl.BlockSpec((D_MODEL, tn), lambda s: (0, mm_tile(s)[1])),
            pl.BlockSpec((None, 3, tm, LANES), rot_map),
        ],
        out_specs=[
            pl.BlockSpec((tm, tn), ep_tile),
            pl.BlockSpec((None, MAX_DIL, tm // MAX_DIL, tn), r16_map),
        ],
        out_shape=[
            jax.ShapeDtypeStruct((m, D_IN), BF16),
            jax.ShapeDtypeStruct((batch, MAX_DIL, seq // MAX_DIL, 3 * D_ATTN), BF16),
        ],
        scratch_shapes=[
            pltpu.VMEM((tm, D_MODEL), BF16),
            pltpu.VMEM((tn // LANES, tm, LANES), F32),
            pltpu.VMEM((tn // LANES, tm, LANES), F32),
            pltpu.VMEM((tn // LANES, tm, LANES), F32),
        ],
        compiler_params=pltpu.CompilerParams(
            dimension_semantics=("arbitrary",),
            vmem_limit_bytes=VMEM_LIMIT),
        name="in_projection",
    )(x2, w_in_b, rot)


def _attend(q, k, vx, bias):
    s = lax.dot_general(q, k, (((1,), (1,)), ((), ())), preferred_element_type=F32) + bias
    m = jnp.max(s, axis=1, keepdims=True)
    p = jnp.exp(s - m)
    acc = jnp.dot(p.astype(BF16), vx, preferred_element_type=F32)
    den = acc[:, HEAD_DIM:]
    return acc[:, :HEAD_DIM] / den, m + jnp.log(den)


def _attn_kernel(q_ref, k_ref, v_ref, g_ref, q16_ref, k16_ref, v16_ref, bias_ref, out_ref,
                 o1_ref, l1_ref, o16_ref, l16_ref, onat_ref, vx_ref, vx16_ref, *, seq):
    W = N_KEYS
    n_sub16 = seq // MAX_DIL
    blocks = seq // W
    blocks16 = n_sub16 // W
    blocks4 = seq // 4 // W
    chunk4 = W // 4
    step = 512

    def fill_vx(t, carry):
        rows = pl.ds(pl.multiple_of(t * step, step), step)
        ones = jnp.ones((step, HEAD_DIM), BF16)
        vx_ref[rows, :HEAD_DIM] = v_ref[rows, :]
        vx_ref[rows, HEAD_DIM:] = ones
        vx16_ref[rows, :HEAD_DIM] = v16_ref[rows, :]
        vx16_ref[rows, HEAD_DIM:] = ones
        return carry

    lax.fori_loop(0, seq // step, fill_vx, 0)

    def banded_pass(qr, kr, vr, o_ref, l_ref, blocks_per_seq):
        def body(it, carry):
            for u in range(ATTN_UNROLL):
                g = it * ATTN_UNROLL + u
                kstart = pl.multiple_of(jnp.maximum(g - 1, 0) * W, W)
                first = (g % blocks_per_seq) == 0
                bidx = jnp.where(g == 0, 2, jnp.where(first, 1, 0))
                qrows = pl.ds(pl.multiple_of(g * W, W), W)
                krows = pl.ds(kstart, 2 * W)
                o, lse = _attend(qr[qrows, :], kr[krows, :], vr[krows, :], bias_ref[bidx])
                o_ref[qrows, :] = o
                l_ref[qrows, :] = lse
            return carry

        lax.fori_loop(0, blocks // ATTN_UNROLL, body, 0)

    banded_pass(q_ref, k_ref, vx_ref, o1_ref, l1_ref, blocks)
    banded_pass(q16_ref, k16_ref, vx16_ref, o16_ref, l16_ref, blocks16)

    def d4_merge_store(r4, j, o4, lse4):
        for c in range(4):
            cls = 4 * c + r4
            rows16 = pl.ds(pl.multiple_of(cls * n_sub16 + j * chunk4, chunk4), chunk4)
            rows_nat = pl.ds(j * chunk4 * MAX_DIL + cls, chunk4, stride=MAX_DIL)
            oc4 = o4[c * chunk4:(c + 1) * chunk4, :]
            lc4 = lse4[c * chunk4:(c + 1) * chunk4, :]
            oc16, lc16 = o16_ref[rows16, :], l16_ref[rows16, :]
            oc1, lc1 = o1_ref[rows_nat, :], l1_ref[rows_nat, :]
            mx = jnp.maximum(jnp.maximum(lc1, lc4), lc16)
            w1, w4, w16 = jnp.exp(lc1 - mx), jnp.exp(lc4 - mx), jnp.exp(lc16 - mx)
            mixed = (w1 * oc1 + w4 * oc4 + w16 * oc16) / (w1 + w4 + w16)
            onat_ref[rows_nat, :] = mixed

    def d4_body(it, carry):
        for u in range(ATTN_UNROLL):
            g = it * ATTN_UNROLL + u
            r4 = g // blocks4
            j = g % blocks4
            qs = pl.multiple_of(j * chunk4, chunk4)
            ks = pl.multiple_of(jnp.maximum(j - 1, 0) * chunk4, chunk4)

            def gather(ref, start, size):
                return jnp.concatenate(
                    [ref[pl.ds(pl.multiple_of((4 * c + r4) * n_sub16 + start, chunk4), size), :]
                     for c in range(4)], axis=0)

            bidx = jnp.where(j == 0, 4, 3)
            o4, lse4 = _attend(gather(q16_ref, qs, chunk4), gather(k16_ref, ks, 2 * chunk4),
                               gather(vx16_ref, ks, 2 * chunk4), bias_ref[bidx])
            d4_merge_store(r4, j, o4, lse4)
        return carry

    lax.fori_loop(0, blocks // ATTN_UNROLL, d4_body, 0)

    def out_body(t, carry):
        rows = pl.ds(pl.multiple_of(t * step, step), step)
        out_ref[rows, :] = (onat_ref[rows, :] * g_ref[rows, :].astype(F32)).astype(BF16)
        return carry

    lax.fori_loop(0, seq // step, out_body, 0)


def _attention_biases():
    W = N_KEYS
    ninf = -jnp.inf
    qi = jnp.arange(W)[:, None]
    kj = jnp.arange(2 * W)[None, :]
    dist = W + qi - kj
    full = (dist >= 0) & (dist <= W)
    first = (dist >= 0) & (kj >= W)
    first0 = kj <= qi
    cq = W // 4
    mq = 4 * (qi % cq) + qi // cq
    mk = 4 * (kj % (2 * cq) - cq) + kj // (2 * cq)
    full4 = (mq - mk >= 0) & (mq - mk <= W)
    mk0 = 4 * (kj % (2 * cq)) + kj // (2 * cq)
    first4 = mk0 <= mq
    masks = jnp.stack([full, first, first0, full4, first4])
    return jnp.where(masks, 0.0, ninf).astype(F32)


def _attention(h_nat, qkv16, batch, seq):
    bias = _attention_biases()
    blk = lambda off: pl.BlockSpec((None, seq, LANES), lambda b, h: (b, 0, off + h))
    gate_off = (3 * D_ATTN + D_POOL) // LANES
    kern = functools.partial(_attn_kernel, seq=seq)
    return pl.pallas_call(
        kern,
        grid=(batch, N_HEADS),
        in_specs=[blk(0), blk(N_HEADS), blk(2 * N_HEADS), blk(gate_off),
                  blk(0), blk(N_HEADS), blk(2 * N_HEADS),
                  pl.BlockSpec((5, N_KEYS, 2 * N_KEYS), lambda b, h: (0, 0, 0))],
        out_specs=pl.BlockSpec((None, seq, LANES), lambda b, h: (b, 0, h)),
        out_shape=jax.ShapeDtypeStruct((batch, seq, D_ATTN), BF16),
        scratch_shapes=[pltpu.VMEM((seq, LANES), F32) for _ in range(5)]
        + [pltpu.VMEM((seq, 2 * HEAD_DIM), BF16) for _ in range(2)],
        compiler_params=pltpu.CompilerParams(
            dimension_semantics=("arbitrary", "arbitrary"),
            vmem_limit_bytes=VMEM_LIMIT),
        name="dilated_attention",
    )(h_nat, h_nat, h_nat, h_nat, qkv16, qkv16, qkv16, bias)


def _out_kernel(ya_ref, u_ref, uh_ref, gp_ref, wp_ref, ps_ref, wo_ref, x_ref, gain_ref,
                bias_ref, out_ref, pad_ref, y_ref, *, tm):
    i = pl.program_id(1)
    pad_ref[0:POOL_HALO, :] = jnp.where(i > 0, uh_ref[...].astype(F32), 0.0)
    pad_ref[POOL_HALO:, :] = u_ref[...].astype(F32)
    y_ref[:, 0:D_ATTN] = ya_ref[...]

    pos = i * tm + lax.broadcasted_iota(jnp.int32, (tm, 1), 0)
    for g, window in enumerate(POOL_WINDOWS):
        cols = slice(g * POOL_GROUP_DIM, (g + 1) * POOL_GROUP_DIM)
        tot = pad_ref[POOL_HALO:, cols]
        for back in range(1, window):
            tot = tot + pad_ref[POOL_HALO - back:POOL_HALO - back + tm, cols]
        count = jnp.minimum(pos + 1, window).astype(F32)
        pooled = tot / count - pad_ref[POOL_HALO:, cols]
        po = jnp.dot(pooled.astype(BF16), wp_ref[g], preferred_element_type=F32)
        po = po * ps_ref[:, cols] * gp_ref[:, cols].astype(F32)
        y_ref[:, D_ATTN + g * POOL_GROUP_DIM:D_ATTN + (g + 1) * POOL_GROUP_DIM] = po.astype(BF16)

    proj = jnp.dot(y_ref[...], wo_ref[...], preferred_element_type=F32)
    z = DEEPNORM_ALPHA * x_ref[...] + proj
    mu = jnp.mean(z, axis=-1, keepdims=True)
    zc = z - mu
    var = jnp.mean(zc * zc, axis=-1, keepdims=True)
    out_ref[...] = zc * lax.rsqrt(var + LN_EPS) * gain_ref[...] + bias_ref[...]


def _output(y_attn, h_nat, w_pool_b, pool_scale, w_out_b, x, gain, bias, batch, seq):
    tm = OUT_TM
    u_blk = 3 * D_ATTN // D_POOL
    gp_blk = (3 * D_ATTN + D_POOL + D_ATTN) // D_POOL
    halo_per_tile = tm // POOL_HALO
    const = lambda shape: pl.BlockSpec(shape, lambda b, i: (0,) * len(shape))
    kern = functools.partial(_out_kernel, tm=tm)
    return pl.pallas_call(
        kern,
        grid=(batch, seq // tm),
        in_specs=[
            pl.BlockSpec((None, tm, D_ATTN), lambda b, i: (b, i, 0)),
            pl.BlockSpec((None, tm, D_POOL), lambda b, i: (b, i, u_blk)),
            pl.BlockSpec((None, POOL_HALO, D_POOL),
                         lambda b, i: (b, jnp.maximum(i * halo_per_tile - 1, 0), u_blk)),
            pl.BlockSpec((None, tm, D_POOL), lambda b, i: (b, i, gp_blk)),
            const((len(POOL_WINDOWS), POOL_GROUP_DIM, POOL_GROUP_DIM)),
            const((1, D_POOL)),
            const((D_MODEL, D_MODEL)),
            pl.BlockSpec((None, tm, D_MODEL), lambda b, i: (b, i, 0)),
            const((1, D_MODEL)),
            const((1, D_MODEL)),
        ],
        out_specs=pl.BlockSpec((None, tm, D_MODEL), lambda b, i: (b, i, 0)),
        out_shape=jax.ShapeDtypeStruct((batch, seq, D_MODEL), F32),
        scratch_shapes=[
            pltpu.VMEM((tm + POOL_HALO, D_POOL), F32),
            pltpu.VMEM((tm, D_MODEL), BF16),
        ],
        compiler_params=pltpu.CompilerParams(
            dimension_semantics=("arbitrary", "arbitrary"),
            vmem_limit_bytes=VMEM_LIMIT),
        name="pool_out_deepnorm",
    )(y_attn, h_nat, h_nat, h_nat, w_pool_b, pool_scale, w_out_b, x, gain, bias)


def _rotary_tables(seq):
    inv_freq = ROPE_THETA ** (-(2.0 * jnp.arange(ROPE_HALF, dtype=F32)) / ROPE_DIM)
    ang = jnp.arange(seq, dtype=jnp.int32).astype(F32)[:, None] * inv_freq[None, :]
    cos, sin = jnp.cos(ang), jnp.sin(ang)
    zeros = jnp.zeros((seq, HEAD_DIM - ROPE_DIM), F32)
    zh = jnp.zeros((seq, ROPE_HALF), F32)
    c = jnp.concatenate([cos, cos, jnp.ones_like(zeros)], axis=1)
    a = jnp.concatenate([zh, sin, zeros], axis=1)
    b = jnp.concatenate([-sin, zh, zeros], axis=1)
    k_tab = jnp.stack([c, a, b])
    return jnp.stack([k_tab * (HEAD_DIM ** -0.5), k_tab])


def kernel(x, w_in, w_pool, pool_scale, w_out, ln_gain, ln_bias):
    batch, seq, d_model = x.shape
    assert d_model == D_MODEL and seq % (MAX_DIL * N_KEYS) == 0 and seq % PROJ_TM == 0
    assert w_in.shape[0] == 1, "single layer"
    rot = _rotary_tables(seq)
    h_nat, qkv16 = _projection(x.reshape(batch * seq, D_MODEL), w_in[0].astype(BF16), rot,
                               batch, seq)
    h_nat = h_nat.reshape(batch, seq, D_IN)
    y_attn = _attention(h_nat, qkv16.reshape(batch, seq, 3 * D_ATTN), batch, seq)
    return _output(y_attn, h_nat, w_pool[0].astype(BF16), pool_scale.astype(F32),
                   w_out[0].astype(BF16), x, ln_gain.astype(F32), ln_bias.astype(F32),
                   batch, seq)
```

```python
import functools

import jax
import jax.numpy as jnp
from jax import lax
from jax.experimental import pallas as pl
from jax.experimental.pallas import tpu as pltpu

D_MODEL = 2048
D_ATTN = 1024
D_POOL = 1024
HEAD_DIM = 128
N_HEADS = D_ATTN // HEAD_DIM
ROPE_DIM = HEAD_DIM // 4
ROPE_HALF = ROPE_DIM // 2
ROPE_THETA = 500000.0
N_KEYS = 128
MAX_DIL = 16
POOL_WINDOWS = (2, 4, 8, 16)
POOL_GROUP_DIM = D_POOL // len(POOL_WINDOWS)
POOL_HALO = 16
D_IN = 3 * D_ATTN + D_POOL + D_MODEL
LN_EPS = 1e-5
DEEPNORM_ALPHA = 2.0 ** 0.25

LANES = 128
VMEM_LIMIT = 56 * 1024 * 1024

SLAB_Q, SLAB_K, SLAB_V = 0, N_HEADS, 2 * N_HEADS
SLAB_U = 3 * D_ATTN // LANES
SLAB_GATE_ATTN = (3 * D_ATTN + D_POOL) // LANES
SLAB_GATE_POOL = (3 * D_ATTN + D_POOL + D_ATTN) // LANES
N_SLABS = D_IN // LANES
POOL_SLABS = D_POOL // LANES

PROJ_TM = 1024
PROJ_TN = 512
OUT_TM = 512
ATTN_UNROLL = 8

BF16 = jnp.bfloat16
F32 = jnp.float32


def _proj_kernel(x_ref, w_ref, rot_ref, nat_ref, r16_ref, xb_ref, raw0_ref, raw1_ref, tmp_ref, *,
                 tm, tn, n_tiles):
    s = pl.program_id(0)
    heads = tn // LANES
    n_col = D_IN // tn
    n_q = D_ATTN // tn
    n_qkv = 3 * n_q
    n_gate0 = n_qkv + D_POOL // tn
    sub = tm // MAX_DIL
    col_e = jnp.maximum(s - 1, 0) % n_col

    @pl.when((s % n_col == 0) & (s < n_tiles))
    def _():
        xb_ref[...] = x_ref[...].astype(BF16)

    raw = (raw0_ref, raw1_ref)

    def matmul_to(raw_ref):
        res = jnp.dot(xb_ref[...], w_ref[...], preferred_element_type=F32)
        for h in range(heads):
            raw_ref[h] = res[:, h * LANES:(h + 1) * LANES]

    def qkv_epilogue(raw_ref):
        is_v = col_e >= 2 * n_q
        rc = 128
        for c0 in range(0, tm, rc):
            rows = slice(c0, c0 + rc)
            ct, at, bt = rot_ref[0, rows, :], rot_ref[1, rows, :], rot_ref[2, rows, :]
            for h in range(heads):
                t = raw_ref[h, rows, :]
                rot = (t * ct + pltpu.roll(t, ROPE_HALF, 1) * at
                       + pltpu.roll(t, LANES - ROPE_HALF, 1) * bt)
                t = jnp.where(is_v, t, rot)
                raw_ref[h, rows, :] = t
                nat_ref[h, rows, :] = t.astype(BF16)
        quarter = tm // 4
        for h in range(heads):
            for r4 in range(4):
                tmp_ref[h, r4 * quarter:(r4 + 1) * quarter, :] = (
                    raw_ref[h, pl.ds(r4, quarter, stride=4), :])
        for r4 in range(4):
            for c in range(4):
                for h in range(heads):
                    t = tmp_ref[h, pl.ds(r4 * quarter + c, sub, stride=4), :]
                    r16_ref[h, 4 * c + r4] = t.astype(BF16)

    def rest_epilogue(raw_ref):
        is_gate = col_e >= n_gate0
        for h in range(heads):
            g = raw_ref[h]
            silu = g / (1.0 + jnp.exp(-g))
            nat_ref[h] = jnp.where(is_gate, silu, g).astype(BF16)

    interior = (s >= 1) & (s < n_tiles)
    for parity in (0, 1):
        here = interior & (s % 2 == parity)

        @pl.when(here & (col_e < n_qkv))
        def _():
            qkv_epilogue(raw[1 - parity])
            matmul_to(raw[parity])

        @pl.when(here & (col_e >= n_qkv))
        def _():
            rest_epilogue(raw[1 - parity])
            matmul_to(raw[parity])

    @pl.when(s == 0)
    def _():
        matmul_to(raw[0])

    @pl.when(s == n_tiles)
    def _():
        rest_epilogue(raw[(n_tiles - 1) % 2])


def _projection(x2, w_tiles, rot, batch, seq):
    tm, tn = PROJ_TM, PROJ_TN
    m = x2.shape[0]
    heads = tn // LANES
    tiles_per_seq = seq // tm
    n_col = D_IN // tn
    n_q = D_ATTN // tn
    last_qkv = 3 * n_q - 1
    n_tiles = (m // tm) * n_col
    assert n_col % 2 == 0
    kern = functools.partial(_proj_kernel, tm=tm, tn=tn, n_tiles=n_tiles)

    def mm_tile(s):
        t = jnp.minimum(s, n_tiles - 1)
        return t // n_col, t % n_col

    def ep_tile(s):
        t = jnp.maximum(s - 1, 0)
        return t // n_col, t % n_col

    def rot_map(s):
        i, j = ep_tile(s)
        return (jnp.minimum(j // n_q, 1), 0, i % tiles_per_seq, 0)

    def nat_map(s):
        i, j = ep_tile(s)
        return (j, i, 0)

    def r16_map(s):
        i, j = ep_tile(s)
        return (jnp.minimum(j, last_qkv), i // tiles_per_seq, 0, i % tiles_per_seq, 0)

    return pl.pallas_call(
        kern,
        grid=(n_tiles + 1,),
        in_specs=[
            pl.BlockSpec((tm, D_MODEL), lambda s: (mm_tile(s)[0], 0)),
            pl.BlockSpec((None, D_MODEL, tn), lambda s: (mm_tile(s)[1], 0, 0)),
            pl.BlockSpec((None, 3, tm, LANES), rot_map),
        ],
        out_specs=[
            pl.BlockSpec((heads, tm, LANES), nat_map),
            pl.BlockSpec((heads, None, MAX_DIL, tm // MAX_DIL, LANES), r16_map),
        ],
        out_shape=[
            jax.ShapeDtypeStruct((N_SLABS, m, LANES), BF16),
            jax.ShapeDtypeStruct((3 * N_HEADS, batch, MAX_DIL, seq // MAX_DIL, LANES), BF16),
        ],
        scratch_shapes=[
            pltpu.VMEM((tm, D_MODEL), BF16),
            pltpu.VMEM((heads, tm, LANES), F32),
            pltpu.VMEM((heads, tm, LANES), F32),
            pltpu.VMEM((heads, tm, LANES), F32),
        ],
        compiler_params=pltpu.CompilerParams(
            dimension_semantics=("arbitrary",),
            vmem_limit_bytes=VMEM_LIMIT),
        name="in_projection",
    )(x2, w_tiles, rot)


def _attend(q, k, vx, bias):
    s = lax.dot_general(q, k, (((1,), (1,)), ((), ())), preferred_element_type=F32) + bias
    m = jnp.max(s, axis=1, keepdims=True)
    p = jnp.exp(s - m)
    acc = jnp.dot(p.astype(BF16), vx, preferred_element_type=F32)
    den = acc[:, HEAD_DIM:]
    return acc[:, :HEAD_DIM] / den, m + jnp.log(den)


def _attn_kernel(q_ref, k_ref, v_ref, g_ref, q16_ref, k16_ref, v16_ref, bias_ref, out_ref,
                 o1_ref, l1_ref, o16_ref, l16_ref, onat_ref, vx_ref, vx16_ref, *, seq):
    W = N_KEYS
    n_sub16 = seq // MAX_DIL
    blocks = seq // W
    blocks16 = n_sub16 // W
    blocks4 = seq // 4 // W
    chunk4 = W // 4
    step = 512

    def fill_vx(t, carry):
        rows = pl.ds(pl.multiple_of(t * step, step), step)
        ones = jnp.ones((step, HEAD_DIM), BF16)
        vx_ref[rows, :HEAD_DIM] = v_ref[rows, :]
        vx_ref[rows, HEAD_DIM:] = ones
        vx16_ref[rows, :HEAD_DIM] = v16_ref[rows, :]
        vx16_ref[rows, HEAD_DIM:] = ones
        return carry

    lax.fori_loop(0, seq // step, fill_vx, 0)

    def banded_pass(qr, kr, vr, o_ref, l_ref, blocks_per_seq):
        def body(it, carry):
            for u in range(ATTN_UNROLL):
                g = it * ATTN_UNROLL + u
                kstart = pl.multiple_of(jnp.maximum(g - 1, 0) * W, W)
                first = (g % blocks_per_seq) == 0
                bidx = jnp.where(g == 0, 2, jnp.where(first, 1, 0))
                qrows = pl.ds(pl.multiple_of(g * W, W), W)
                krows = pl.ds(kstart, 2 * W)
                o, lse = _attend(qr[qrows, :], kr[krows, :], vr[krows, :], bias_ref[bidx])
                o_ref[qrows, :] = o
                l_ref[qrows, :] = lse
            return carry

        lax.fori_loop(0, blocks // ATTN_UNROLL, body, 0)

    banded_pass(q_ref, k_ref, vx_ref, o1_ref, l1_ref, blocks)
    banded_pass(q16_ref, k16_ref, vx16_ref, o16_ref, l16_ref, blocks16)

    def d4_merge_store(r4, j, o4, lse4):
        for c in range(4):
            cls = 4 * c + r4
            rows16 = pl.ds(pl.multiple_of(cls * n_sub16 + j * chunk4, chunk4), chunk4)
            rows_nat = pl.ds(j * chunk4 * MAX_DIL + cls, chunk4, stride=MAX_DIL)
            oc4 = o4[c * chunk4:(c + 1) * chunk4, :]
            lc4 = lse4[c * chunk4:(c + 1) * chunk4, :]
            oc16, lc16 = o16_ref[rows16, :], l16_ref[rows16, :]
            oc1, lc1 = o1_ref[rows_nat, :], l1_ref[rows_nat, :]
            mx = jnp.maximum(jnp.maximum(lc1, lc4), lc16)
            w1, w4, w16 = jnp.exp(lc1 - mx), jnp.exp(lc4 - mx), jnp.exp(lc16 - mx)
            mixed = (w1 * oc1 + w4 * oc4 + w16 * oc16) / (w1 + w4 + w16)
            onat_ref[rows_nat, :] = mixed

    def d4_body(it, carry):
        for u in range(ATTN_UNROLL):
            g = it * ATTN_UNROLL + u
            r4 = g // blocks4
            j = g % blocks4
            qs = pl.multiple_of(j * chunk4, chunk4)
            ks = pl.multiple_of(jnp.maximum(j - 1, 0) * chunk4, chunk4)

            def gather(ref, start, size):
                return jnp.concatenate(
                    [ref[pl.ds(pl.multiple_of((4 * c + r4) * n_sub16 + start, chunk4), size), :]
                     for c in range(4)], axis=0)

            bidx = jnp.where(j == 0, 4, 3)
            o4, lse4 = _attend(gather(q16_ref, qs, chunk4), gather(k16_ref, ks, 2 * chunk4),
                               gather(vx16_ref, ks, 2 * chunk4), bias_ref[bidx])
            d4_merge_store(r4, j, o4, lse4)
        return carry

    lax.fori_loop(0, blocks // ATTN_UNROLL, d4_body, 0)

    def out_body(t, carry):
        rows = pl.ds(pl.multiple_of(t * step, step), step)
        out_ref[rows, :] = (onat_ref[rows, :] * g_ref[rows, :].astype(F32)).astype(BF16)
        return carry

    lax.fori_loop(0, seq // step, out_body, 0)


def _attention_biases():
    W = N_KEYS
    ninf = -jnp.inf
    qi = jnp.arange(W)[:, None]
    kj = jnp.arange(2 * W)[None, :]
    dist = W + qi - kj
    full = (dist >= 0) & (dist <= W)
    first = (dist >= 0) & (kj >= W)
    first0 = kj <= qi
    cq = W // 4
    mq = 4 * (qi % cq) + qi // cq
    mk = 4 * (kj % (2 * cq) - cq) + kj // (2 * cq)
    full4 = (mq - mk >= 0) & (mq - mk <= W)
    mk0 = 4 * (kj % (2 * cq)) + kj // (2 * cq)
    first4 = mk0 <= mq
    masks = jnp.stack([full, first, first0, full4, first4])
    return jnp.where(masks, 0.0, ninf).astype(F32)


def _attention(slabs, qkv16, batch, seq):
    bias = _attention_biases()
    nat = lambda off: pl.BlockSpec((None, seq, LANES), lambda b, h: (off + h, b, 0))
    r16 = lambda off: pl.BlockSpec((None, None, seq, LANES), lambda b, h: (off + h, b, 0, 0))
    kern = functools.partial(_attn_kernel, seq=seq)
    return pl.pallas_call(
        kern,
        grid=(batch, N_HEADS),
        in_specs=[nat(SLAB_Q), nat(SLAB_K), nat(SLAB_V), nat(SLAB_GATE_ATTN),
                  r16(SLAB_Q), r16(SLAB_K), r16(SLAB_V),
                  pl.BlockSpec((5, N_KEYS, 2 * N_KEYS), lambda b, h: (0, 0, 0))],
        out_specs=pl.BlockSpec((None, seq, LANES), lambda b, h: (h, b, 0)),
        out_shape=jax.ShapeDtypeStruct((N_HEADS, batch * seq, LANES), BF16),
        scratch_shapes=[pltpu.VMEM((seq, LANES), F32) for _ in range(5)]
        + [pltpu.VMEM((seq, 2 * HEAD_DIM), BF16) for _ in range(2)],
        compiler_params=pltpu.CompilerParams(
            dimension_semantics=("arbitrary", "arbitrary"),
            vmem_limit_bytes=VMEM_LIMIT),
        name="dilated_attention",
    )(slabs, slabs, slabs, slabs, qkv16, qkv16, qkv16, bias)


def _out_kernel(ya_ref, u_ref, uh_ref, gp_ref, wp_ref, ps_ref, wo_ref, x_ref, gain_ref,
                bias_ref, out_ref, pad_ref, y_ref, *, tm):
    i = pl.program_id(1)
    pos = i * tm + lax.broadcasted_iota(jnp.int32, (tm, 1), 0)
    for k in range(N_HEADS):
        y_ref[:, k * LANES:(k + 1) * LANES] = ya_ref[k]
    for k in range(POOL_SLABS):
        pad_ref[k, 0:POOL_HALO, :] = jnp.where(i > 0, uh_ref[k].astype(F32), 0.0)
        pad_ref[k, POOL_HALO:, :] = u_ref[k].astype(F32)

    slabs_per_group = POOL_GROUP_DIM // LANES
    for g, window in enumerate(POOL_WINDOWS):
        count = jnp.minimum(pos + 1, window).astype(F32)
        pooled = []
        for k in range(g * slabs_per_group, (g + 1) * slabs_per_group):
            tot = pad_ref[k, POOL_HALO:, :]
            for back in range(1, window):
                tot = tot + pad_ref[k, POOL_HALO - back:POOL_HALO - back + tm, :]
            pooled.append((tot / count - pad_ref[k, POOL_HALO:, :]).astype(BF16))
        po = jnp.dot(jnp.concatenate(pooled, axis=1), wp_ref[g], preferred_element_type=F32)
        for n, k in enumerate(range(g * slabs_per_group, (g + 1) * slabs_per_group)):
            cols = slice(k * LANES, (k + 1) * LANES)
            yk = po[:, n * LANES:(n + 1) * LANES] * ps_ref[:, cols] * gp_ref[k].astype(F32)
            y_ref[:, D_ATTN + k * LANES:D_ATTN + (k + 1) * LANES] = yk.astype(BF16)

    proj = jnp.dot(y_ref[...], wo_ref[...], preferred_element_type=F32)
    z = DEEPNORM_ALPHA * x_ref[...] + proj
    mu = jnp.mean(z, axis=-1, keepdims=True)
    zc = z - mu
    var = jnp.mean(zc * zc, axis=-1, keepdims=True)
    out_ref[...] = zc * lax.rsqrt(var + LN_EPS) * gain_ref[...] + bias_ref[...]


def _output(y_attn, slabs, w_pool_b, pool_scale, w_out_b, x2, gain, bias, batch, seq):
    tm = OUT_TM
    tiles = seq // tm
    halo_per_tile = tm // POOL_HALO
    const = lambda shape: pl.BlockSpec(shape, lambda b, i: (0,) * len(shape))
    group = lambda first: pl.BlockSpec((POOL_SLABS, tm, LANES),
                                       lambda b, i: (first // POOL_SLABS, b * tiles + i, 0))
    kern = functools.partial(_out_kernel, tm=tm)
    return pl.pallas_call(
        kern,
        grid=(batch, tiles),
        in_specs=[
            group(0),
            group(SLAB_U),
            pl.BlockSpec((POOL_SLABS, POOL_HALO, LANES),
                         lambda b, i: (SLAB_U // POOL_SLABS,
                                       jnp.maximum((b * tiles + i) * halo_per_tile - 1, 0), 0)),
            group(SLAB_GATE_POOL),
            const((len(POOL_WINDOWS), POOL_GROUP_DIM, POOL_GROUP_DIM)),
            const((1, D_POOL)),
            const((D_MODEL, D_MODEL)),
            pl.BlockSpec((tm, D_MODEL), lambda b, i: (b * tiles + i, 0)),
            const((1, D_MODEL)),
            const((1, D_MODEL)),
        ],
        out_specs=pl.BlockSpec((tm, D_MODEL), lambda b, i: (b * tiles + i, 0)),
        out_shape=jax.ShapeDtypeStruct((batch * seq, D_MODEL), F32),
        scratch_shapes=[
            pltpu.VMEM((POOL_SLABS, tm + POOL_HALO, LANES), F32),
            pltpu.VMEM((tm, D_MODEL), BF16),
        ],
        compiler_params=pltpu.CompilerParams(
            dimension_semantics=("arbitrary", "arbitrary"),
            vmem_limit_bytes=VMEM_LIMIT),
        name="pool_out_deepnorm",
    )(y_attn, slabs, slabs, slabs, w_pool_b, pool_scale, w_out_b, x2, gain, bias)


def _rotary_tables(seq):
    inv_freq = ROPE_THETA ** (-(2.0 * jnp.arange(ROPE_HALF, dtype=F32)) / ROPE_DIM)
    ang = jnp.arange(seq, dtype=jnp.int32).astype(F32)[:, None] * inv_freq[None, :]
    cos, sin = jnp.cos(ang), jnp.sin(ang)
    zeros = jnp.zeros((seq, HEAD_DIM - ROPE_DIM), F32)
    zh = jnp.zeros((seq, ROPE_HALF), F32)
    c = jnp.concatenate([cos, cos, jnp.ones_like(zeros)], axis=1)
    a = jnp.concatenate([zh, sin, zeros], axis=1)
    b = jnp.concatenate([-sin, zh, zeros], axis=1)
    k_tab = jnp.stack([c, a, b])
    return jnp.stack([k_tab * (HEAD_DIM ** -0.5), k_tab])


def kernel(x, w_in, w_pool, pool_scale, w_out, ln_gain, ln_bias):
    batch, seq, d_model = x.shape
    assert d_model == D_MODEL and seq % (MAX_DIL * N_KEYS) == 0 and seq % PROJ_TM == 0
    assert w_in.shape[0] == 1, "single layer"
    x2 = x.reshape(batch * seq, D_MODEL)
    rot = _rotary_tables(seq)
    w_tiles = w_in[0].astype(BF16).reshape(D_MODEL, D_IN // PROJ_TN, PROJ_TN).transpose(1, 0, 2)
    slabs, qkv16 = _projection(x2, w_tiles, rot, batch, seq)
    qkv16 = qkv16.reshape(3 * N_HEADS, batch, seq, LANES)
    y_attn = _attention(slabs, qkv16, batch, seq)
    out = _output(y_attn, slabs, w_pool[0].astype(BF16), pool_scale.astype(F32),
                  w_out[0].astype(BF16), x2, ln_gain.astype(F32), ln_bias.astype(F32),
                  batch, seq)
    return out.reshape(batch, seq, D_MODEL)
```

```python
import functools

import jax
import jax.numpy as jnp
import numpy as np
from jax import lax
from jax.experimental import pallas as pl
from jax.experimental.pallas import tpu as pltpu

D_MODEL = 2048
D_ATTN = 1024
D_POOL = 1024
HEAD_DIM = 128
N_HEADS = D_ATTN // HEAD_DIM
ROPE_DIM = HEAD_DIM // 4
ROPE_HALF = ROPE_DIM // 2
ROPE_THETA = 500000.0
N_KEYS = 128
MAX_DIL = 16
POOL_WINDOWS = (2, 4, 8, 16)
POOL_GROUP_DIM = D_POOL // len(POOL_WINDOWS)
POOL_HALO = 16
D_IN = 3 * D_ATTN + D_POOL + D_MODEL
LN_EPS = 1e-5
DEEPNORM_ALPHA = 2.0 ** 0.25

LANES = 128
VMEM_LIMIT = 56 * 1024 * 1024

SLAB_Q, SLAB_K, SLAB_V = 0, N_HEADS, 2 * N_HEADS
SLAB_U = 3 * D_ATTN // LANES
SLAB_GATE_ATTN = (3 * D_ATTN + D_POOL) // LANES
SLAB_GATE_POOL = (3 * D_ATTN + D_POOL + D_ATTN) // LANES
N_SLABS = D_IN // LANES
POOL_SLABS = D_POOL // LANES

PROJ_TM = 1024
PROJ_TN = 512
OUT_TM = 512
ATTN_UNROLL = 8

BF16 = jnp.bfloat16
F32 = jnp.float32


def _proj_kernel(x_ref, w_ref, rot_ref, nat_ref, r16_ref, xb_ref, raw0_ref, raw1_ref, tmp_ref, *,
                 tm, tn, n_tiles):
    s = pl.program_id(0)
    heads = tn // LANES
    n_col = D_IN // tn
    n_q = D_ATTN // tn
    n_qkv = 3 * n_q
    n_gate0 = n_qkv + D_POOL // tn
    sub = tm // MAX_DIL
    col_e = jnp.maximum(s - 1, 0) % n_col

    @pl.when((s % n_col == 0) & (s < n_tiles))
    def _():
        xb_ref[...] = x_ref[...].astype(BF16)

    raw = (raw0_ref, raw1_ref)

    def matmul_to(raw_ref):
        res = jnp.dot(xb_ref[...], w_ref[...], preferred_element_type=F32)
        for h in range(heads):
            raw_ref[h] = res[:, h * LANES:(h + 1) * LANES]

    def qkv_epilogue(raw_ref):
        is_v = col_e >= 2 * n_q
        q_scale = jnp.where(col_e < n_q, HEAD_DIM ** -0.5, 1.0).astype(F32)
        rc = 128
        for c0 in range(0, tm, rc):
            rows = slice(c0, c0 + rc)
            ct = rot_ref[0, rows, :] * q_scale
            at = rot_ref[1, rows, :] * q_scale
            bt = rot_ref[2, rows, :] * q_scale
            for h in range(heads):
                t = raw_ref[h, rows, :]
                rot = (t * ct + pltpu.roll(t, ROPE_HALF, 1) * at
                       + pltpu.roll(t, LANES - ROPE_HALF, 1) * bt)
                t = jnp.where(is_v, t, rot)
                raw_ref[h, rows, :] = t
                nat_ref[h, rows, :] = t.astype(BF16)
        quarter = tm // 4
        for h in range(heads):
            for r4 in range(4):
                tmp_ref[h, r4 * quarter:(r4 + 1) * quarter, :] = (
                    raw_ref[h, pl.ds(r4, quarter, stride=4), :])
        for r4 in range(4):
            for c in range(4):
                for h in range(heads):
                    t = tmp_ref[h, pl.ds(r4 * quarter + c, sub, stride=4), :]
                    r16_ref[h, 4 * c + r4] = t.astype(BF16)

    def rest_epilogue(raw_ref):
        is_gate = col_e >= n_gate0
        for h in range(heads):
            g = raw_ref[h]
            silu = g / (1.0 + jnp.exp(-g))
            nat_ref[h] = jnp.where(is_gate, silu, g).astype(BF16)

    interior = (s >= 1) & (s < n_tiles)
    for parity in (0, 1):
        here = interior & (s % 2 == parity)

        @pl.when(here & (col_e < n_qkv))
        def _():
            qkv_epilogue(raw[1 - parity])
            matmul_to(raw[parity])

        @pl.when(here & (col_e >= n_qkv))
        def _():
            rest_epilogue(raw[1 - parity])
            matmul_to(raw[parity])

    @pl.when(s == 0)
    def _():
        matmul_to(raw[0])

    @pl.when(s == n_tiles)
    def _():
        rest_epilogue(raw[(n_tiles - 1) % 2])


def _projection(x2, w_tiles, rot, batch, seq):
    tm, tn = PROJ_TM, PROJ_TN
    m = x2.shape[0]
    heads = tn // LANES
    tiles_per_seq = seq // tm
    n_col = D_IN // tn
    n_q = D_ATTN // tn
    last_qkv = 3 * n_q - 1
    n_tiles = (m // tm) * n_col
    assert n_col % 2 == 0
    kern = functools.partial(_proj_kernel, tm=tm, tn=tn, n_tiles=n_tiles)

    def mm_tile(s):
        t = jnp.minimum(s, n_tiles - 1)
        return t // n_col, t % n_col

    def ep_tile(s):
        t = jnp.maximum(s - 1, 0)
        return t // n_col, t % n_col

    def rot_map(s):
        return (0, ep_tile(s)[0] % tiles_per_seq, 0)

    def nat_map(s):
        i, j = ep_tile(s)
        return (j, i, 0)

    def r16_map(s):
        i, j = ep_tile(s)
        return (jnp.minimum(j, last_qkv), i // tiles_per_seq, 0, i % tiles_per_seq, 0)

    return pl.pallas_call(
        kern,
        grid=(n_tiles + 1,),
        in_specs=[
            pl.BlockSpec((tm, D_MODEL), lambda s: (mm_tile(s)[0], 0)),
            pl.BlockSpec((D_MODEL, tn), lambda s: (0, mm_tile(s)[1])),
            pl.BlockSpec((3, tm, LANES), rot_map),
        ],
        out_specs=[
            pl.BlockSpec((heads, tm, LANES), nat_map),
            pl.BlockSpec((heads, None, MAX_DIL, tm // MAX_DIL, LANES), r16_map),
        ],
        out_shape=[
            jax.ShapeDtypeStruct((N_SLABS, m, LANES), BF16),
            jax.ShapeDtypeStruct((3 * N_HEADS, batch, MAX_DIL, seq // MAX_DIL, LANES), BF16),
        ],
        scratch_shapes=[
            pltpu.VMEM((tm, D_MODEL), BF16),
            pltpu.VMEM((heads, tm, LANES), F32),
            pltpu.VMEM((heads, tm, LANES), F32),
            pltpu.VMEM((heads, tm, LANES), F32),
        ],
        compiler_params=pltpu.CompilerParams(
            dimension_semantics=("arbitrary",),
            vmem_limit_bytes=VMEM_LIMIT),
        name="in_projection",
    )(x2, w_tiles, rot)


def _attend(q, k, vx, bias):
    s = lax.dot_general(q, k, (((1,), (1,)), ((), ())), preferred_element_type=F32) + bias
    m = jnp.max(s, axis=1, keepdims=True)
    p = jnp.exp(s - m)
    acc = jnp.dot(p.astype(BF16), vx, preferred_element_type=F32)
    den = acc[:, HEAD_DIM:]
    return acc[:, :HEAD_DIM] / den, m + jnp.log(den)


def _attn_kernel(q_ref, k_ref, v_ref, g_ref, q16_ref, k16_ref, v16_ref, bias_ref, out_ref,
                 o1_ref, l1_ref, o16_ref, l16_ref, onat_ref, vx_ref, vx16_ref, *, seq):
    W = N_KEYS
    n_sub16 = seq // MAX_DIL
    blocks = seq // W
    blocks16 = n_sub16 // W
    blocks4 = seq // 4 // W
    chunk4 = W // 4
    step = 512

    def fill_vx(t, carry):
        rows = pl.ds(pl.multiple_of(t * step, step), step)
        ones = jnp.ones((step, HEAD_DIM), BF16)
        vx_ref[rows, :HEAD_DIM] = v_ref[rows, :]
        vx_ref[rows, HEAD_DIM:] = ones
        vx16_ref[rows, :HEAD_DIM] = v16_ref[rows, :]
        vx16_ref[rows, HEAD_DIM:] = ones
        return carry

    lax.fori_loop(0, seq // step, fill_vx, 0)

    def banded_pass(qr, kr, vr, o_ref, l_ref, blocks_per_seq):
        def body(it, carry):
            for u in range(ATTN_UNROLL):
                g = it * ATTN_UNROLL + u
                kstart = pl.multiple_of(jnp.maximum(g - 1, 0) * W, W)
                first = (g % blocks_per_seq) == 0
                bidx = jnp.where(g == 0, 2, jnp.where(first, 1, 0))
                qrows = pl.ds(pl.multiple_of(g * W, W), W)
                krows = pl.ds(kstart, 2 * W)
                o, lse = _attend(qr[qrows, :], kr[krows, :], vr[krows, :], bias_ref[bidx])
                o_ref[qrows, :] = o
                l_ref[qrows, :] = lse
            return carry

        lax.fori_loop(0, blocks // ATTN_UNROLL, body, 0)

    banded_pass(q_ref, k_ref, vx_ref, o1_ref, l1_ref, blocks)
    banded_pass(q16_ref, k16_ref, vx16_ref, o16_ref, l16_ref, blocks16)

    def d4_merge_store(r4, j, o4, lse4):
        for c in range(4):
            cls = 4 * c + r4
            rows16 = pl.ds(pl.multiple_of(cls * n_sub16 + j * chunk4, chunk4), chunk4)
            rows_nat = pl.ds(j * chunk4 * MAX_DIL + cls, chunk4, stride=MAX_DIL)
            oc4 = o4[c * chunk4:(c + 1) * chunk4, :]
            lc4 = lse4[c * chunk4:(c + 1) * chunk4, :]
            oc16, lc16 = o16_ref[rows16, :], l16_ref[rows16, :]
            oc1, lc1 = o1_ref[rows_nat, :], l1_ref[rows_nat, :]
            mx = jnp.maximum(jnp.maximum(lc1, lc4), lc16)
            w1, w4, w16 = jnp.exp(lc1 - mx), jnp.exp(lc4 - mx), jnp.exp(lc16 - mx)
            mixed = (w1 * oc1 + w4 * oc4 + w16 * oc16) / (w1 + w4 + w16)
            onat_ref[rows_nat, :] = mixed

    def d4_body(it, carry):
        for u in range(ATTN_UNROLL):
            g = it * ATTN_UNROLL + u
            r4 = g // blocks4
            j = g % blocks4
            qs = pl.multiple_of(j * chunk4, chunk4)
            ks = pl.multiple_of(jnp.maximum(j - 1, 0) * chunk4, chunk4)

            def gather(ref, start, size):
                return jnp.concatenate(
                    [ref[pl.ds(pl.multiple_of((4 * c + r4) * n_sub16 + start, chunk4), size), :]
                     for c in range(4)], axis=0)

            bidx = jnp.where(j == 0, 4, 3)
            o4, lse4 = _attend(gather(q16_ref, qs, chunk4), gather(k16_ref, ks, 2 * chunk4),
                               gather(vx16_ref, ks, 2 * chunk4), bias_ref[bidx])
            d4_merge_store(r4, j, o4, lse4)
        return carry

    lax.fori_loop(0, blocks // ATTN_UNROLL, d4_body, 0)

    def out_body(t, carry):
        rows = pl.ds(pl.multiple_of(t * step, step), step)
        out_ref[rows, :] = (onat_ref[rows, :] * g_ref[rows, :].astype(F32)).astype(BF16)
        return carry

    lax.fori_loop(0, seq // step, out_body, 0)


def _attention_biases():
    W = N_KEYS
    ninf = -jnp.inf
    qi = jnp.arange(W)[:, None]
    kj = jnp.arange(2 * W)[None, :]
    dist = W + qi - kj
    full = (dist >= 0) & (dist <= W)
    first = (dist >= 0) & (kj >= W)
    first0 = kj <= qi
    cq = W // 4
    mq = 4 * (qi % cq) + qi // cq
    mk = 4 * (kj % (2 * cq) - cq) + kj // (2 * cq)
    full4 = (mq - mk >= 0) & (mq - mk <= W)
    mk0 = 4 * (kj % (2 * cq)) + kj // (2 * cq)
    first4 = mk0 <= mq
    masks = jnp.stack([full, first, first0, full4, first4])
    return jnp.where(masks, 0.0, ninf).astype(F32)


def _attention(slabs, qkv16, batch, seq):
    bias = _attention_biases()
    nat = lambda off: pl.BlockSpec((None, seq, LANES), lambda b, h: (off + h, b, 0))
    r16 = lambda off: pl.BlockSpec((None, None, seq, LANES), lambda b, h: (off + h, b, 0, 0))
    kern = functools.partial(_attn_kernel, seq=seq)
    return pl.pallas_call(
        kern,
        grid=(batch, N_HEADS),
        in_specs=[nat(SLAB_Q), nat(SLAB_K), nat(SLAB_V), nat(SLAB_GATE_ATTN),
                  r16(SLAB_Q), r16(SLAB_K), r16(SLAB_V),
                  pl.BlockSpec((5, N_KEYS, 2 * N_KEYS), lambda b, h: (0, 0, 0))],
        out_specs=pl.BlockSpec((None, seq, LANES), lambda b, h: (h, b, 0)),
        out_shape=jax.ShapeDtypeStruct((N_HEADS, batch * seq, LANES), BF16),
        scratch_shapes=[pltpu.VMEM((seq, LANES), F32) for _ in range(5)]
        + [pltpu.VMEM((seq, 2 * HEAD_DIM), BF16) for _ in range(2)],
        compiler_params=pltpu.CompilerParams(
            dimension_semantics=("arbitrary", "arbitrary"),
            vmem_limit_bytes=VMEM_LIMIT),
        name="dilated_attention",
    )(slabs, slabs, slabs, slabs, qkv16, qkv16, qkv16, bias)


def _out_kernel(ya_ref, u_ref, uh_ref, gp_ref, wp_ref, ps_ref, wo_ref, x_ref, gain_ref,
                bias_ref, out_ref, y_ref, raw_ref, *, tm, n_tiles, tiles_per_seq):
    g = pl.program_id(0)
    raw = (raw_ref.at[0], raw_ref.at[1])
    slabs_per_group = POOL_GROUP_DIM // LANES

    n_col = 4
    tn = D_MODEL // n_col
    rc = 16
    n_norm = tm // rc

    def attn_half(raw_ref, c):
        cols = slice(c * tn, (c + 1) * tn)
        raw_ref[:, cols] = jnp.dot(y_ref[:, :D_ATTN], wo_ref[:D_ATTN, cols],
                                   preferred_element_type=F32)

    def pool_half(raw_ref, c):
        cols = slice(c * tn, (c + 1) * tn)
        raw_ref[:, cols] += jnp.dot(y_ref[:, D_ATTN:], wo_ref[D_ATTN:, cols],
                                    preferred_element_type=F32)

    def pool_group(grp):
        window = POOL_WINDOWS[grp]
        i = jnp.minimum(g, n_tiles - 1) % tiles_per_seq
        pos = i * tm + lax.broadcasted_iota(jnp.int32, (tm, 1), 0)
        inv_count = 1.0 / jnp.minimum(pos + 1, window).astype(F32)
        pooled = []
        for k in range(grp * slabs_per_group, (grp + 1) * slabs_per_group):
            u = u_ref[k].astype(F32)
            halo = jnp.where(i > 0, uh_ref[k].astype(F32), 0.0)
            tot = jnp.concatenate([halo, u], axis=0)
            d = 1
            while d < window:
                tot = tot + pltpu.roll(tot, d, 0)
                d *= 2
            pooled.append((tot[POOL_HALO:] * inv_count - u).astype(BF16))
        po = jnp.dot(jnp.concatenate(pooled, axis=1), wp_ref[grp], preferred_element_type=F32)
        for n, k in enumerate(range(grp * slabs_per_group, (grp + 1) * slabs_per_group)):
            cols = slice(k * LANES, (k + 1) * LANES)
            yk = po[:, n * LANES:(n + 1) * LANES] * ps_ref[:, cols] * gp_ref[k].astype(F32)
            y_ref[:, D_ATTN + k * LANES:D_ATTN + (k + 1) * LANES] = yk.astype(BF16)

    def normalise(raw_ref, chunk):
        rows = slice(chunk * rc, (chunk + 1) * rc)
        z = DEEPNORM_ALPHA * x_ref[rows, :] + raw_ref[rows, :]
        mu = jnp.mean(z, axis=-1, keepdims=True)
        zc = z - mu
        var = jnp.mean(zc * zc, axis=-1, keepdims=True)
        out_ref[rows, :] = zc * lax.rsqrt(var + LN_EPS) * gain_ref[...] + bias_ref[...]

    def step(raw_cur, raw_prev):
        per_piece = n_norm // (2 * n_col)
        norm_chunks = iter(range(n_norm))

        def some_norm(next_cols):
            if raw_prev is not None:
                for _ in range(per_piece):
                    chunk = next(norm_chunks)
                    normalise(raw_prev, chunk)
                if next_cols is not None:
                    done = out_ref[chunk * rc:(chunk + 1) * rc, :LANES]
                    zero = pltpu.bitcast(
                        lax.shift_right_logical(pltpu.bitcast(done, jnp.uint32), jnp.uint32(32)),
                        F32)
                    tile = y_ref[:rc, next_cols]
                    y_ref[:rc, next_cols] = (tile.astype(F32) + zero).astype(BF16)

        if raw_cur is not None:
            for k in range(N_HEADS):
                y_ref[:, k * LANES:(k + 1) * LANES] = ya_ref[k]
            attn_in, pool_in = slice(0, LANES), slice(D_ATTN, D_ATTN + LANES)
            for c in range(n_col):
                attn_half(raw_cur, c)
                pool_group(c)
                some_norm(attn_in if c + 1 < n_col else pool_in)
            for c in range(n_col):
                pool_half(raw_cur, c)
                some_norm(pool_in if c + 1 < n_col else None)
        else:
            for chunk in norm_chunks:
                normalise(raw_prev, chunk)

    assert len(POOL_WINDOWS) == n_col and n_norm % (2 * n_col) == 0
    interior = (g >= 1) & (g < n_tiles)
    for parity in (0, 1):
        @pl.when(interior & (g % 2 == parity))
        def _():
            step(raw[parity], raw[1 - parity])

    @pl.when(g == 0)
    def _():
        step(raw[0], None)

    @pl.when(g == n_tiles)
    def _():
        step(None, raw[(n_tiles - 1) % 2])


def _output(y_attn, slabs, w_pool_b, pool_scale, w_out_b, x2, gain, bias, batch, seq):
    tm = OUT_TM
    tiles_per_seq = seq // tm
    n_tiles = batch * tiles_per_seq
    halo_per_tile = tm // POOL_HALO
    front = lambda g: jnp.minimum(g, n_tiles - 1)
    back = lambda g: jnp.maximum(g - 1, 0)
    const = lambda shape: pl.BlockSpec(shape, lambda g: (0,) * len(shape),
                                       pipeline_mode=pl.Buffered(1))
    group = lambda first: pl.BlockSpec((POOL_SLABS, tm, LANES),
                                       lambda g: (first // POOL_SLABS, front(g), 0))
    kern = functools.partial(_out_kernel, tm=tm, n_tiles=n_tiles, tiles_per_seq=tiles_per_seq)
    return pl.pallas_call(
        kern,
        grid=(n_tiles + 1,),
        in_specs=[
            group(0),
            group(SLAB_U),
            pl.BlockSpec((POOL_SLABS, POOL_HALO, LANES),
                         lambda g: (SLAB_U // POOL_SLABS,
                                    jnp.maximum(front(g) * halo_per_tile - 1, 0), 0)),
            group(SLAB_GATE_POOL),
            const((len(POOL_WINDOWS), POOL_GROUP_DIM, POOL_GROUP_DIM)),
            const((1, D_POOL)),
            const((D_MODEL, D_MODEL)),
            pl.BlockSpec((tm, D_MODEL), lambda g: (back(g), 0)),
            const((1, D_MODEL)),
            const((1, D_MODEL)),
        ],
        out_specs=pl.BlockSpec((tm, D_MODEL), lambda g: (back(g), 0)),
        out_shape=jax.ShapeDtypeStruct((batch * seq, D_MODEL), F32),
        scratch_shapes=[
            pltpu.VMEM((tm, D_MODEL), BF16),
            pltpu.VMEM((2, tm, D_MODEL), F32),
        ],
        compiler_params=pltpu.CompilerParams(
            dimension_semantics=("arbitrary",),
            vmem_limit_bytes=VMEM_LIMIT),
        name="pool_out_deepnorm",
    )(y_attn, slabs, slabs, slabs, w_pool_b, pool_scale, w_out_b, x2, gain, bias)


def _rotary_tables(seq):
    inv_freq = ROPE_THETA ** (-(2.0 * np.arange(ROPE_HALF, dtype=np.float64)) / ROPE_DIM)
    ang = np.arange(seq, dtype=np.float64)[:, None] * inv_freq[None, :]
    cos, sin = np.cos(ang), np.sin(ang)
    zeros = np.zeros((seq, HEAD_DIM - ROPE_DIM))
    zh = np.zeros((seq, ROPE_HALF))
    c = np.concatenate([cos, cos, np.ones_like(zeros)], axis=1)
    a = np.concatenate([zh, sin, zeros], axis=1)
    b = np.concatenate([-sin, zh, zeros], axis=1)
    return jnp.asarray(np.stack([c, a, b]), dtype=F32)


def kernel(x, w_in, w_pool, pool_scale, w_out, ln_gain, ln_bias):
    batch, seq, d_model = x.shape
    assert d_model == D_MODEL and seq % (MAX_DIL * N_KEYS) == 0 and seq % PROJ_TM == 0
    assert w_in.shape[0] == 1, "single layer"
    x2 = x.reshape(batch * seq, D_MODEL)
    rot = _rotary_tables(seq)
    slabs, qkv16 = _projection(x2, w_in[0].astype(BF16), rot, batch, seq)
    qkv16 = qkv16.reshape(3 * N_HEADS, batch, seq, LANES)
    y_attn = _attention(slabs, qkv16, batch, seq)
    out = _output(y_attn, slabs, w_pool[0].astype(BF16), pool_scale.astype(F32),
                  w_out[0].astype(BF16), x2, ln_gain.astype(F32), ln_bias.astype(F32),
                  batch, seq)
    return out.reshape(batch, seq, D_MODEL)
```

```python
import functools

import jax
import jax.numpy as jnp
import numpy as np
from jax import lax
from jax.experimental import pallas as pl
from jax.experimental.pallas import tpu as pltpu

D_MODEL = 2048
D_ATTN = 1024
D_POOL = 1024
HEAD_DIM = 128
N_HEADS = D_ATTN // HEAD_DIM
ROPE_DIM = HEAD_DIM // 4
ROPE_HALF = ROPE_DIM // 2
ROPE_THETA = 500000.0
N_KEYS = 128
MAX_DIL = 16
POOL_WINDOWS = (2, 4, 8, 16)
POOL_GROUP_DIM = D_POOL // len(POOL_WINDOWS)
POOL_HALO = 16
D_IN = 3 * D_ATTN + D_POOL + D_MODEL
LN_EPS = 1e-5
DEEPNORM_ALPHA = 2.0 ** 0.25

LANES = 128
VMEM_LIMIT = 56 * 1024 * 1024

SLAB_Q, SLAB_K, SLAB_V = 0, N_HEADS, 2 * N_HEADS
SLAB_U = 3 * D_ATTN // LANES
SLAB_GATE_ATTN = (3 * D_ATTN + D_POOL) // LANES
SLAB_GATE_POOL = (3 * D_ATTN + D_POOL + D_ATTN) // LANES
N_SLABS = D_IN // LANES
POOL_SLABS = D_POOL // LANES

PROJ_TM = 1024
PROJ_TN = 512
OUT_TM = 512
ATTN_UNROLL = 32

BF16 = jnp.bfloat16
F32 = jnp.float32


def _proj_kernel(x_ref, w_ref, rot_ref, nat_ref, r16_ref, xb_ref, raw0_ref, raw1_ref, tmp_ref, *,
                 tm, tn, n_tiles):
    s = pl.program_id(0)
    heads = tn // LANES
    n_col = D_IN // tn
    n_q = D_ATTN // tn
    n_qkv = 3 * n_q
    n_gate0 = n_qkv + D_POOL // tn
    sub = tm // MAX_DIL
    col_e = jnp.maximum(s - 1, 0) % n_col

    @pl.when((s % n_col == 0) & (s < n_tiles))
    def _():
        xb_ref[...] = x_ref[...].astype(BF16)

    raw = (raw0_ref, raw1_ref)

    def matmul_to(raw_ref):
        res = jnp.dot(xb_ref[...], w_ref[...], preferred_element_type=F32)
        for h in range(heads):
            raw_ref[h] = res[:, h * LANES:(h + 1) * LANES]

    def qkv_epilogue(raw_ref):
        is_v = col_e >= 2 * n_q
        q_scale = jnp.where(col_e < n_q, HEAD_DIM ** -0.5, 1.0).astype(F32)
        rc = 128
        for c0 in range(0, tm, rc):
            rows = slice(c0, c0 + rc)
            ct = rot_ref[0, rows, :] * q_scale
            at = rot_ref[1, rows, :] * q_scale
            bt = rot_ref[2, rows, :] * q_scale
            for h in range(heads):
                t = raw_ref[h, rows, :]
                rot = (t * ct + pltpu.roll(t, ROPE_HALF, 1) * at
                       + pltpu.roll(t, LANES - ROPE_HALF, 1) * bt)
                t = jnp.where(is_v, t, rot)
                raw_ref[h, rows, :] = t
                nat_ref[h, rows, :] = t.astype(BF16)
        quarter = tm // 4
        for h in range(heads):
            for r4 in range(4):
                tmp_ref[h, r4 * quarter:(r4 + 1) * quarter, :] = (
                    raw_ref[h, pl.ds(r4, quarter, stride=4), :])
        for r4 in range(4):
            for c in range(4):
                for h in range(heads):
                    t = tmp_ref[h, pl.ds(r4 * quarter + c, sub, stride=4), :]
                    r16_ref[h, 4 * c + r4] = t.astype(BF16)

    def rest_epilogue(raw_ref):
        is_gate = col_e >= n_gate0
        for h in range(heads):
            g = raw_ref[h]
            silu = g / (1.0 + jnp.exp(-g))
            nat_ref[h] = jnp.where(is_gate, silu, g).astype(BF16)

    interior = (s >= 1) & (s < n_tiles)
    for parity in (0, 1):
        here = interior & (s % 2 == parity)

        @pl.when(here & (col_e < n_qkv))
        def _():
            qkv_epilogue(raw[1 - parity])
            matmul_to(raw[parity])

        @pl.when(here & (col_e >= n_qkv))
        def _():
            rest_epilogue(raw[1 - parity])
            matmul_to(raw[parity])

    @pl.when(s == 0)
    def _():
        matmul_to(raw[0])

    @pl.when(s == n_tiles)
    def _():
        rest_epilogue(raw[(n_tiles - 1) % 2])


def _projection(x2, w_tiles, rot, batch, seq):
    tm, tn = PROJ_TM, PROJ_TN
    m = x2.shape[0]
    heads = tn // LANES
    tiles_per_seq = seq // tm
    n_col = D_IN // tn
    n_q = D_ATTN // tn
    last_qkv = 3 * n_q - 1
    n_tiles = (m // tm) * n_col
    assert n_col % 2 == 0
    kern = functools.partial(_proj_kernel, tm=tm, tn=tn, n_tiles=n_tiles)

    def mm_tile(s):
        t = jnp.minimum(s, n_tiles - 1)
        return t // n_col, t % n_col

    def ep_tile(s):
        t = jnp.maximum(s - 1, 0)
        return t // n_col, t % n_col

    def rot_map(s):
        return (0, ep_tile(s)[0] % tiles_per_seq, 0)

    def nat_map(s):
        i, j = ep_tile(s)
        return (j, i, 0)

    def r16_map(s):
        i, j = ep_tile(s)
        return (jnp.minimum(j, last_qkv), i // tiles_per_seq, 0, i % tiles_per_seq, 0)

    return pl.pallas_call(
        kern,
        grid=(n_tiles + 1,),
        in_specs=[
            pl.BlockSpec((tm, D_MODEL), lambda s: (mm_tile(s)[0], 0)),
            pl.BlockSpec((D_MODEL, tn), lambda s: (0, mm_tile(s)[1])),
            pl.BlockSpec((3, tm, LANES), rot_map),
        ],
        out_specs=[
            pl.BlockSpec((heads, tm, LANES), nat_map),
            pl.BlockSpec((heads, None, MAX_DIL, tm // MAX_DIL, LANES), r16_map),
        ],
        out_shape=[
            jax.ShapeDtypeStruct((N_SLABS, m, LANES), BF16),
            jax.ShapeDtypeStruct((3 * N_HEADS, batch, MAX_DIL, seq // MAX_DIL, LANES), BF16),
        ],
        scratch_shapes=[
            pltpu.VMEM((tm, D_MODEL), BF16),
            pltpu.VMEM((heads, tm, LANES), F32),
            pltpu.VMEM((heads, tm, LANES), F32),
            pltpu.VMEM((heads, tm, LANES), F32),
        ],
        compiler_params=pltpu.CompilerParams(
            dimension_semantics=("arbitrary",),
            vmem_limit_bytes=VMEM_LIMIT),
        name="in_projection",
    )(x2, w_tiles, rot)


def _attend(q, k, vx, bias):
    s = lax.dot_general(q, k, (((1,), (1,)), ((), ())), preferred_element_type=F32) + bias
    m = jnp.max(s, axis=1, keepdims=True)
    p = jnp.exp(s - m)
    acc = jnp.dot(p.astype(BF16), vx, preferred_element_type=F32)
    den = acc[:, HEAD_DIM:]
    return acc[:, :HEAD_DIM] / den, m + jnp.log(den)


def _attn_kernel(q_ref, k_ref, v_ref, g_ref, q16_ref, k16_ref, v16_ref, bias_ref, out_ref,
                 o1_ref, l1_ref, o16_ref, l16_ref, onat_ref, vx_ref, vx16_ref, *, seq):
    W = N_KEYS
    n_sub16 = seq // MAX_DIL
    blocks = seq // W
    blocks16 = n_sub16 // W
    blocks4 = seq // 4 // W
    chunk4 = W // 4
    step = 512

    def fill_vx(t, carry):
        rows = pl.ds(pl.multiple_of(t * step, step), step)
        ones = jnp.ones((step, HEAD_DIM), BF16)
        vx_ref[rows, :HEAD_DIM] = v_ref[rows, :]
        vx_ref[rows, HEAD_DIM:] = ones
        vx16_ref[rows, :HEAD_DIM] = v16_ref[rows, :]
        vx16_ref[rows, HEAD_DIM:] = ones
        return carry

    lax.fori_loop(0, seq // step, fill_vx, 0)

    def banded_pass(qr, kr, vr, o_ref, l_ref, blocks_per_seq):
        def body(it, carry):
            for u in range(ATTN_UNROLL):
                g = it * ATTN_UNROLL + u
                kstart = pl.multiple_of(jnp.maximum(g - 1, 0) * W, W)
                first = (g % blocks_per_seq) == 0
                bidx = jnp.where(g == 0, 2, jnp.where(first, 1, 0))
                qrows = pl.ds(pl.multiple_of(g * W, W), W)
                krows = pl.ds(kstart, 2 * W)
                o, lse = _attend(qr[qrows, :], kr[krows, :], vr[krows, :], bias_ref[bidx])
                o_ref[qrows, :] = o
                l_ref[qrows, :] = lse
            return carry

        lax.fori_loop(0, blocks // ATTN_UNROLL, body, 0)

    banded_pass(q_ref, k_ref, vx_ref, o1_ref, l1_ref, blocks)
    banded_pass(q16_ref, k16_ref, vx16_ref, o16_ref, l16_ref, blocks16)

    def d4_merge_store(r4, j, o4, lse4):
        for c in range(4):
            cls = 4 * c + r4
            rows16 = pl.ds(pl.multiple_of(cls * n_sub16 + j * chunk4, chunk4), chunk4)
            rows_nat = pl.ds(j * chunk4 * MAX_DIL + cls, chunk4, stride=MAX_DIL)
            oc4 = o4[c * chunk4:(c + 1) * chunk4, :]
            lc4 = lse4[c * chunk4:(c + 1) * chunk4, :]
            oc16, lc16 = o16_ref[rows16, :], l16_ref[rows16, :]
            oc1, lc1 = o1_ref[rows_nat, :], l1_ref[rows_nat, :]
            mx = jnp.maximum(jnp.maximum(lc1, lc4), lc16)
            w1, w4, w16 = jnp.exp(lc1 - mx), jnp.exp(lc4 - mx), jnp.exp(lc16 - mx)
            mixed = (w1 * oc1 + w4 * oc4 + w16 * oc16) / (w1 + w4 + w16)
            onat_ref[rows_nat, :] = mixed

    def d4_body(it, carry):
        for u in range(ATTN_UNROLL):
            g = it * ATTN_UNROLL + u
            r4 = g // blocks4
            j = g % blocks4
            qs = pl.multiple_of(j * chunk4, chunk4)
            ks = pl.multiple_of(jnp.maximum(j - 1, 0) * chunk4, chunk4)

            def gather(ref, start, size):
                return jnp.concatenate(
                    [ref[pl.ds(pl.multiple_of((4 * c + r4) * n_sub16 + start, chunk4), size), :]
                     for c in range(4)], axis=0)

            bidx = jnp.where(j == 0, 4, 3)
            o4, lse4 = _attend(gather(q16_ref, qs, chunk4), gather(k16_ref, ks, 2 * chunk4),
                               gather(vx16_ref, ks, 2 * chunk4), bias_ref[bidx])
            d4_merge_store(r4, j, o4, lse4)
        return carry

    lax.fori_loop(0, blocks // ATTN_UNROLL, d4_body, 0)

    def out_body(t, carry):
        rows = pl.ds(pl.multiple_of(t * step, step), step)
        out_ref[rows, :] = (onat_ref[rows, :] * g_ref[rows, :].astype(F32)).astype(BF16)
        return carry

    lax.fori_loop(0, seq // step, out_body, 0)


def _attention_biases():
    W = N_KEYS
    ninf = -jnp.inf
    qi = jnp.arange(W)[:, None]
    kj = jnp.arange(2 * W)[None, :]
    dist = W + qi - kj
    full = (dist >= 0) & (dist <= W)
    first = (dist >= 0) & (kj >= W)
    first0 = kj <= qi
    cq = W // 4
    mq = 4 * (qi % cq) + qi // cq
    mk = 4 * (kj % (2 * cq) - cq) + kj // (2 * cq)
    full4 = (mq - mk >= 0) & (mq - mk <= W)
    mk0 = 4 * (kj % (2 * cq)) + kj // (2 * cq)
    first4 = mk0 <= mq
    masks = jnp.stack([full, first, first0, full4, first4])
    return jnp.where(masks, 0.0, ninf).astype(F32)


def _attention(slabs, qkv16, batch, seq):
    bias = _attention_biases()
    nat = lambda off: pl.BlockSpec((None, seq, LANES), lambda b, h: (off + h, b, 0))
    r16 = lambda off: pl.BlockSpec((None, None, seq, LANES), lambda b, h: (off + h, b, 0, 0))
    kern = functools.partial(_attn_kernel, seq=seq)
    return pl.pallas_call(
        kern,
        grid=(batch, N_HEADS),
        in_specs=[nat(SLAB_Q), nat(SLAB_K), nat(SLAB_V), nat(SLAB_GATE_ATTN),
                  r16(SLAB_Q), r16(SLAB_K), r16(SLAB_V),
                  pl.BlockSpec((5, N_KEYS, 2 * N_KEYS), lambda b, h: (0, 0, 0))],
        out_specs=pl.BlockSpec((None, seq, LANES), lambda b, h: (h, b, 0)),
        out_shape=jax.ShapeDtypeStruct((N_HEADS, batch * seq, LANES), BF16),
        scratch_shapes=[pltpu.VMEM((seq, LANES), F32) for _ in range(5)]
        + [pltpu.VMEM((seq, 2 * HEAD_DIM), BF16) for _ in range(2)],
        compiler_params=pltpu.CompilerParams(
            dimension_semantics=("arbitrary", "arbitrary"),
            vmem_limit_bytes=VMEM_LIMIT),
        name="dilated_attention",
    )(slabs, slabs, slabs, slabs, qkv16, qkv16, qkv16, bias)


def _out_kernel(ya_ref, u_ref, uh_ref, gp_ref, wp_ref, ps_ref, wo_ref, x_ref, gain_ref,
                bias_ref, out_ref, y_ref, raw_ref, *, tm, n_tiles, tiles_per_seq):
    g = pl.program_id(0)
    raw = (raw_ref.at[0], raw_ref.at[1])
    slabs_per_group = POOL_GROUP_DIM // LANES

    n_col = 4
    tn = D_MODEL // n_col
    rc = 16
    n_norm = tm // rc

    def attn_half(raw_ref, c):
        cols = slice(c * tn, (c + 1) * tn)
        raw_ref[:, cols] = jnp.dot(y_ref[:, :D_ATTN], wo_ref[:D_ATTN, cols],
                                   preferred_element_type=F32)

    def pool_half(raw_ref, c):
        cols = slice(c * tn, (c + 1) * tn)
        raw_ref[:, cols] += jnp.dot(y_ref[:, D_ATTN:], wo_ref[D_ATTN:, cols],
                                    preferred_element_type=F32)

    def pool_group(grp):
        window = POOL_WINDOWS[grp]
        i = jnp.minimum(g, n_tiles - 1) % tiles_per_seq
        pos = i * tm + lax.broadcasted_iota(jnp.int32, (tm, 1), 0)
        inv_count = 1.0 / jnp.minimum(pos + 1, window).astype(F32)
        pooled = []
        for k in range(grp * slabs_per_group, (grp + 1) * slabs_per_group):
            u = u_ref[k].astype(F32)
            halo = jnp.where(i > 0, uh_ref[k].astype(F32), 0.0)
            tot = jnp.concatenate([halo, u], axis=0)
            d = 1
            while d < window:
                tot = tot + pltpu.roll(tot, d, 0)
                d *= 2
            pooled.append((tot[POOL_HALO:] * inv_count - u).astype(BF16))
        po = jnp.dot(jnp.concatenate(pooled, axis=1), wp_ref[grp], preferred_element_type=F32)
        for n, k in enumerate(range(grp * slabs_per_group, (grp + 1) * slabs_per_group)):
            cols = slice(k * LANES, (k + 1) * LANES)
            yk = po[:, n * LANES:(n + 1) * LANES] * ps_ref[:, cols] * gp_ref[k].astype(F32)
            y_ref[:, D_ATTN + k * LANES:D_ATTN + (k + 1) * LANES] = yk.astype(BF16)

    def runtime_zero(v):
        return pltpu.bitcast(
            lax.shift_right_logical(pltpu.bitcast(v, jnp.uint32), jnp.uint32(32)), F32)

    def normalise(raw_ref, chunk, after=None):
        rows = slice(chunk * rc, (chunk + 1) * rc)
        z = DEEPNORM_ALPHA * x_ref[rows, :] + raw_ref[rows, :]
        mu = jnp.mean(z, axis=-1, keepdims=True)
        if after is not None:
            mu = mu + runtime_zero(after)[:, :1]
        zc = z - mu
        var = jnp.mean(zc * zc, axis=-1, keepdims=True)
        o = zc * lax.rsqrt(var + LN_EPS) * gain_ref[...] + bias_ref[...]
        out_ref[rows, :] = o
        fold = o[:, :LANES]
        for j in range(1, D_MODEL // LANES):
            fold = fold + o[:, j * LANES:(j + 1) * LANES]
        return fold

    def step(raw_cur, raw_prev):
        norm_chunks = iter(range(n_norm))
        spots = n_col - 1
        groups = [n_norm // spots + (1 if c < n_norm % spots else 0) for c in range(spots)]

        def norm_group(count):
            done = None
            for _ in range(count):
                fold = normalise(raw_prev, next(norm_chunks))
                done = fold if done is None else done + fold
            tile = y_ref[:rc, D_ATTN:D_ATTN + LANES]
            y_ref[:rc, D_ATTN:D_ATTN + LANES] = (
                tile.astype(F32) + runtime_zero(done)).astype(BF16)

        if raw_cur is not None:
            for k in range(N_HEADS):
                y_ref[:, k * LANES:(k + 1) * LANES] = ya_ref[k]
            for c in range(n_col):
                attn_half(raw_cur, c)
                pool_group(c)
            for c in range(n_col):
                pool_half(raw_cur, c)
                if raw_prev is not None and c < spots:
                    norm_group(groups[c])
        else:
            for chunk in norm_chunks:
                normalise(raw_prev, chunk)

    assert len(POOL_WINDOWS) == n_col
    interior = (g >= 1) & (g < n_tiles)
    for parity in (0, 1):
        @pl.when(interior & (g % 2 == parity))
        def _():
            step(raw[parity], raw[1 - parity])

    @pl.when(g == 0)
    def _():
        step(raw[0], None)

    @pl.when(g == n_tiles)
    def _():
        step(None, raw[(n_tiles - 1) % 2])


def _output(y_attn, slabs, w_pool_b, pool_scale, w_out_b, x2, gain, bias, batch, seq):
    tm = OUT_TM
    tiles_per_seq = seq // tm
    n_tiles = batch * tiles_per_seq
    halo_per_tile = tm // POOL_HALO
    front = lambda g: jnp.minimum(g, n_tiles - 1)
    back = lambda g: jnp.maximum(g - 1, 0)
    const = lambda shape: pl.BlockSpec(shape, lambda g: (0,) * len(shape),
                                       pipeline_mode=pl.Buffered(1))
    group = lambda first: pl.BlockSpec((POOL_SLABS, tm, LANES),
                                       lambda g: (first // POOL_SLABS, front(g), 0))
    kern = functools.partial(_out_kernel, tm=tm, n_tiles=n_tiles, tiles_per_seq=tiles_per_seq)
    return pl.pallas_call(
        kern,
        grid=(n_tiles + 1,),
        in_specs=[
            group(0),
            group(SLAB_U),
            pl.BlockSpec((POOL_SLABS, POOL_HALO, LANES),
                         lambda g: (SLAB_U // POOL_SLABS,
                                    jnp.maximum(front(g) * halo_per_tile - 1, 0), 0)),
            group(SLAB_GATE_POOL),
            const((len(POOL_WINDOWS), POOL_GROUP_DIM, POOL_GROUP_DIM)),
            const((1, D_POOL)),
            const((D_MODEL, D_MODEL)),
            pl.BlockSpec((tm, D_MODEL), lambda g: (back(g), 0)),
            const((1, D_MODEL)),
            const((1, D_MODEL)),
        ],
        out_specs=pl.BlockSpec((tm, D_MODEL), lambda g: (back(g), 0)),
        out_shape=jax.ShapeDtypeStruct((batch * seq, D_MODEL), F32),
        scratch_shapes=[
            pltpu.VMEM((tm, D_MODEL), BF16),
            pltpu.VMEM((2, tm, D_MODEL), F32),
        ],
        compiler_params=pltpu.CompilerParams(
            dimension_semantics=("arbitrary",),
            vmem_limit_bytes=VMEM_LIMIT),
        name="pool_out_deepnorm",
    )(y_attn, slabs, slabs, slabs, w_pool_b, pool_scale, w_out_b, x2, gain, bias)


def _rotary_tables(seq):
    inv_freq = ROPE_THETA ** (-(2.0 * np.arange(ROPE_HALF, dtype=np.float64)) / ROPE_DIM)
    ang = np.arange(seq, dtype=np.float64)[:, None] * inv_freq[None, :]
    cos, sin = np.cos(ang), np.sin(ang)
    zeros = np.zeros((seq, HEAD_DIM - ROPE_DIM))
    zh = np.zeros((seq, ROPE_HALF))
    c = np.concatenate([cos, cos, np.ones_like(zeros)], axis=1)
    a = np.concatenate([zh, sin, zeros], axis=1)
    b = np.concatenate([-sin, zh, zeros], axis=1)
    return jnp.asarray(np.stack([c, a, b]), dtype=F32)


def kernel(x, w_in, w_pool, pool_scale, w_out, ln_gain, ln_bias):
    batch, seq, d_model = x.shape
    assert d_model == D_MODEL and seq % (MAX_DIL * N_KEYS) == 0 and seq % PROJ_TM == 0
    assert w_in.shape[0] == 1, "single layer"
    x2 = x.reshape(batch * seq, D_MODEL)
    rot = _rotary_tables(seq)
    slabs, qkv16 = _projection(x2, w_in[0].astype(BF16), rot, batch, seq)
    qkv16 = qkv16.reshape(3 * N_HEADS, batch, seq, LANES)
    y_attn = _attention(slabs, qkv16, batch, seq)
    out = _output(y_attn, slabs, w_pool[0].astype(BF16), pool_scale.astype(F32),
                  w_out[0].astype(BF16), x2, ln_gain.astype(F32), ln_bias.astype(F32),
                  batch, seq)
    return out.reshape(batch, seq, D_MODEL)
```

```python
import functools

import jax
import jax.numpy as jnp
import numpy as np
from jax import lax
from jax.experimental import pallas as pl
from jax.experimental.pallas import tpu as pltpu

D_MODEL = 2048
D_ATTN = 1024
D_POOL = 1024
HEAD_DIM = 128
N_HEADS = D_ATTN // HEAD_DIM
ROPE_DIM = HEAD_DIM // 4
ROPE_HALF = ROPE_DIM // 2
ROPE_THETA = 500000.0
N_KEYS = 128
MAX_DIL = 16
POOL_WINDOWS = (2, 4, 8, 16)
POOL_GROUP_DIM = D_POOL // len(POOL_WINDOWS)
POOL_HALO = 16
D_IN = 3 * D_ATTN + D_POOL + D_MODEL
LN_EPS = 1e-5
DEEPNORM_ALPHA = 2.0 ** 0.25

LANES = 128
VMEM_LIMIT = 56 * 1024 * 1024

SLAB_Q, SLAB_K, SLAB_V = 0, N_HEADS, 2 * N_HEADS
SLAB_U = 3 * D_ATTN // LANES
SLAB_GATE_ATTN = (3 * D_ATTN + D_POOL) // LANES
SLAB_GATE_POOL = (3 * D_ATTN + D_POOL + D_ATTN) // LANES
N_SLABS = D_IN // LANES
POOL_SLABS = D_POOL // LANES

PROJ_TM = 1024
PROJ_TN = 1024
OUT_TM = 512
ATTN_UNROLL = 32

BF16 = jnp.bfloat16
F32 = jnp.float32


def _proj_kernel(x_ref, w_ref, rot_ref, nat_ref, r16_ref, xb_ref, raw0_ref, raw1_ref, tmp_ref, *,
                 tm, tn, n_tiles):
    s = pl.program_id(0)
    heads = tn // LANES
    n_col = D_IN // tn
    n_q = D_ATTN // tn
    n_qkv = 3 * n_q
    n_gate0 = n_qkv + D_POOL // tn
    sub = tm // MAX_DIL
    col_e = jnp.maximum(s - 1, 0) % n_col

    @pl.when((s % n_col == 0) & (s < n_tiles))
    def _():
        xb_ref[...] = x_ref[...].astype(BF16)

    raw = (raw0_ref, raw1_ref)

    def matmul_to(raw_ref):
        res = jnp.dot(xb_ref[...], w_ref[...], preferred_element_type=F32)
        for h in range(heads):
            raw_ref[h] = res[:, h * LANES:(h + 1) * LANES]

    def qkv_epilogue(raw_ref):
        is_v = col_e >= 2 * n_q
        q_scale = jnp.where(col_e < n_q, HEAD_DIM ** -0.5, 1.0).astype(F32)
        rc = 128
        for c0 in range(0, tm, rc):
            rows = slice(c0, c0 + rc)
            ct = rot_ref[0, rows, :] * q_scale
            at = rot_ref[1, rows, :] * q_scale
            bt = rot_ref[2, rows, :] * q_scale
            for h in range(heads):
                t = raw_ref[h, rows, :]
                rot = (t * ct + pltpu.roll(t, ROPE_HALF, 1) * at
                       + pltpu.roll(t, LANES - ROPE_HALF, 1) * bt)
                t = jnp.where(is_v, t, rot)
                raw_ref[h, rows, :] = t
                nat_ref[h, rows, :] = t.astype(BF16)
        quarter = tm // 4
        for h in range(heads):
            for r4 in range(4):
                tmp_ref[h, r4 * quarter:(r4 + 1) * quarter, :] = (
                    raw_ref[h, pl.ds(r4, quarter, stride=4), :])
        for r4 in range(4):
            for c in range(4):
                for h in range(heads):
                    t = tmp_ref[h, pl.ds(r4 * quarter + c, sub, stride=4), :]
                    r16_ref[h, 4 * c + r4] = t.astype(BF16)

    def rest_epilogue(raw_ref):
        is_gate = col_e >= n_gate0
        for h in range(heads):
            g = raw_ref[h]
            silu = g / (1.0 + jnp.exp(-g))
            nat_ref[h] = jnp.where(is_gate, silu, g).astype(BF16)

    interior = (s >= 1) & (s < n_tiles)
    for parity in (0, 1):
        here = interior & (s % 2 == parity)

        @pl.when(here & (col_e < n_qkv))
        def _():
            qkv_epilogue(raw[1 - parity])
            matmul_to(raw[parity])

        @pl.when(here & (col_e >= n_qkv))
        def _():
            rest_epilogue(raw[1 - parity])
            matmul_to(raw[parity])

    @pl.when(s == 0)
    def _():
        matmul_to(raw[0])

    @pl.when(s == n_tiles)
    def _():
        rest_epilogue(raw[(n_tiles - 1) % 2])


def _projection(x2, w_tiles, rot, batch, seq):
    tm, tn = PROJ_TM, PROJ_TN
    m = x2.shape[0]
    heads = tn // LANES
    tiles_per_seq = seq // tm
    n_col = D_IN // tn
    n_q = D_ATTN // tn
    last_qkv = 3 * n_q - 1
    n_tiles = (m // tm) * n_col
    assert n_col % 2 == 0
    kern = functools.partial(_proj_kernel, tm=tm, tn=tn, n_tiles=n_tiles)

    def mm_tile(s):
        t = jnp.minimum(s, n_tiles - 1)
        return t // n_col, t % n_col

    def ep_tile(s):
        t = jnp.maximum(s - 1, 0)
        return t // n_col, t % n_col

    def rot_map(s):
        return (0, ep_tile(s)[0] % tiles_per_seq, 0)

    def nat_map(s):
        i, j = ep_tile(s)
        return (j, i, 0)

    def r16_map(s):
        i, j = ep_tile(s)
        return (jnp.minimum(j, last_qkv), i // tiles_per_seq, 0, i % tiles_per_seq, 0)

    return pl.pallas_call(
        kern,
        grid=(n_tiles + 1,),
        in_specs=[
            pl.BlockSpec((tm, D_MODEL), lambda s: (mm_tile(s)[0], 0)),
            pl.BlockSpec((D_MODEL, tn), lambda s: (0, mm_tile(s)[1])),
            pl.BlockSpec((3, tm, LANES), rot_map),
        ],
        out_specs=[
            pl.BlockSpec((heads, tm, LANES), nat_map),
            pl.BlockSpec((heads, None, MAX_DIL, tm // MAX_DIL, LANES), r16_map),
        ],
        out_shape=[
            jax.ShapeDtypeStruct((N_SLABS, m, LANES), BF16),
            jax.ShapeDtypeStruct((3 * N_HEADS, batch, MAX_DIL, seq // MAX_DIL, LANES), BF16),
        ],
        scratch_shapes=[
            pltpu.VMEM((tm, D_MODEL), BF16),
            pltpu.VMEM((heads, tm, LANES), F32),
            pltpu.VMEM((heads, tm, LANES), F32),
            pltpu.VMEM((heads, tm, LANES), F32),
        ],
        compiler_params=pltpu.CompilerParams(
            dimension_semantics=("arbitrary",),
            vmem_limit_bytes=VMEM_LIMIT),
        name="in_projection",
    )(x2, w_tiles, rot)


def _attend(q, k, vx, bias):
    s = lax.dot_general(q, k, (((1,), (1,)), ((), ())), preferred_element_type=F32) + bias
    m = jnp.max(s, axis=1, keepdims=True)
    p = jnp.exp(s - m)
    acc = jnp.dot(p.astype(BF16), vx, preferred_element_type=F32)
    den = acc[:, HEAD_DIM:]
    return acc[:, :HEAD_DIM] / den, m + jnp.log(den)


def _attn_kernel(q_ref, k_ref, v_ref, g_ref, q16_ref, k16_ref, v16_ref, bias_ref, out_ref,
                 o1_ref, l1_ref, o16_ref, l16_ref, onat_ref, vx_ref, vx16_ref, *, seq):
    W = N_KEYS
    n_sub16 = seq // MAX_DIL
    blocks = seq // W
    blocks16 = n_sub16 // W
    blocks4 = seq // 4 // W
    chunk4 = W // 4
    step = 512

    def fill_vx(t, carry):
        rows = pl.ds(pl.multiple_of(t * step, step), step)
        ones = jnp.ones((step, HEAD_DIM), BF16)
        vx_ref[rows, :HEAD_DIM] = v_ref[rows, :]
        vx_ref[rows, HEAD_DIM:] = ones
        vx16_ref[rows, :HEAD_DIM] = v16_ref[rows, :]
        vx16_ref[rows, HEAD_DIM:] = ones
        return carry

    lax.fori_loop(0, seq // step, fill_vx, 0)

    def banded_pass(qr, kr, vr, o_ref, l_ref, blocks_per_seq):
        def body(it, carry):
            for u in range(ATTN_UNROLL):
                g = it * ATTN_UNROLL + u
                kstart = pl.multiple_of(jnp.maximum(g - 1, 0) * W, W)
                first = (g % blocks_per_seq) == 0
                bidx = jnp.where(g == 0, 2, jnp.where(first, 1, 0))
                qrows = pl.ds(pl.multiple_of(g * W, W), W)
                krows = pl.ds(kstart, 2 * W)
                o, lse = _attend(qr[qrows, :], kr[krows, :], vr[krows, :], bias_ref[bidx])
                o_ref[qrows, :] = o
                l_ref[qrows, :] = lse
            return carry

        lax.fori_loop(0, blocks // ATTN_UNROLL, body, 0)

    banded_pass(q_ref, k_ref, vx_ref, o1_ref, l1_ref, blocks)
    banded_pass(q16_ref, k16_ref, vx16_ref, o16_ref, l16_ref, blocks16)

    def d4_merge_store(r4, j, o4, lse4):
        for c in range(4):
            cls = 4 * c + r4
            rows16 = pl.ds(pl.multiple_of(cls * n_sub16 + j * chunk4, chunk4), chunk4)
            rows_nat = pl.ds(j * chunk4 * MAX_DIL + cls, chunk4, stride=MAX_DIL)
            oc4 = o4[c * chunk4:(c + 1) * chunk4, :]
            lc4 = lse4[c * chunk4:(c + 1) * chunk4, :]
            oc16, lc16 = o16_ref[rows16, :], l16_ref[rows16, :]
            oc1, lc1 = o1_ref[rows_nat, :], l1_ref[rows_nat, :]
            mx = jnp.maximum(jnp.maximum(lc1, lc4), lc16)
            w1, w4, w16 = jnp.exp(lc1 - mx), jnp.exp(lc4 - mx), jnp.exp(lc16 - mx)
            mixed = (w1 * oc1 + w4 * oc4 + w16 * oc16) / (w1 + w4 + w16)
            onat_ref[rows_nat, :] = mixed

    def d4_body(it, carry):
        for u in range(ATTN_UNROLL):
            g = it * ATTN_UNROLL + u
            r4 = g // blocks4
            j = g % blocks4
            qs = pl.multiple_of(j * chunk4, chunk4)
            ks = pl.multiple_of(jnp.maximum(j - 1, 0) * chunk4, chunk4)

            def gather(ref, start, size):
                return jnp.concatenate(
                    [ref[pl.ds(pl.multiple_of((4 * c + r4) * n_sub16 + start, chunk4), size), :]
                     for c in range(4)], axis=0)

            bidx = jnp.where(j == 0, 4, 3)
            o4, lse4 = _attend(gather(q16_ref, qs, chunk4), gather(k16_ref, ks, 2 * chunk4),
                               gather(vx16_ref, ks, 2 * chunk4), bias_ref[bidx])
            d4_merge_store(r4, j, o4, lse4)
        return carry

    lax.fori_loop(0, blocks // ATTN_UNROLL, d4_body, 0)

    def out_body(t, carry):
        rows = pl.ds(pl.multiple_of(t * step, step), step)
        out_ref[rows, :] = (onat_ref[rows, :] * g_ref[rows, :].astype(F32)).astype(BF16)
        return carry

    lax.fori_loop(0, seq // step, out_body, 0)


def _attention_biases():
    W = N_KEYS
    ninf = -jnp.inf
    qi = jnp.arange(W)[:, None]
    kj = jnp.arange(2 * W)[None, :]
    dist = W + qi - kj
    full = (dist >= 0) & (dist <= W)
    first = (dist >= 0) & (kj >= W)
    first0 = kj <= qi
    cq = W // 4
    mq = 4 * (qi % cq) + qi // cq
    mk = 4 * (kj % (2 * cq) - cq) + kj // (2 * cq)
    full4 = (mq - mk >= 0) & (mq - mk <= W)
    mk0 = 4 * (kj % (2 * cq)) + kj // (2 * cq)
    first4 = mk0 <= mq
    masks = jnp.stack([full, first, first0, full4, first4])
    return jnp.where(masks, 0.0, ninf).astype(F32)


def _attention(slabs, qkv16, batch, seq):
    bias = _attention_biases()
    nat = lambda off: pl.BlockSpec((None, seq, LANES), lambda b, h: (off + h, b, 0))
    r16 = lambda off: pl.BlockSpec((None, None, seq, LANES), lambda b, h: (off + h, b, 0, 0))
    kern = functools.partial(_attn_kernel, seq=seq)
    return pl.pallas_call(
        kern,
        grid=(batch, N_HEADS),
        in_specs=[nat(SLAB_Q), nat(SLAB_K), nat(SLAB_V), nat(SLAB_GATE_ATTN),
                  r16(SLAB_Q), r16(SLAB_K), r16(SLAB_V),
                  pl.BlockSpec((5, N_KEYS, 2 * N_KEYS), lambda b, h: (0, 0, 0))],
        out_specs=pl.BlockSpec((None, seq, LANES), lambda b, h: (h, b, 0)),
        out_shape=jax.ShapeDtypeStruct((N_HEADS, batch * seq, LANES), BF16),
        scratch_shapes=[pltpu.VMEM((seq, LANES), F32) for _ in range(5)]
        + [pltpu.VMEM((seq, 2 * HEAD_DIM), BF16) for _ in range(2)],
        compiler_params=pltpu.CompilerParams(
            dimension_semantics=("arbitrary", "arbitrary"),
            vmem_limit_bytes=VMEM_LIMIT),
        name="dilated_attention",
    )(slabs, slabs, slabs, slabs, qkv16, qkv16, qkv16, bias)


def _out_kernel(ya_ref, u_ref, uh_ref, gp_ref, wp_ref, ps_ref, wo_ref, x_ref, gain_ref,
                bias_ref, out_ref, y_ref, raw_ref, *, tm, n_tiles, tiles_per_seq):
    g = pl.program_id(0)
    raw = (raw_ref.at[0], raw_ref.at[1])
    slabs_per_group = POOL_GROUP_DIM // LANES

    n_col = 4
    tn = D_MODEL // n_col
    rc = 16
    n_norm = tm // rc

    def attn_half(raw_ref, c):
        cols = slice(c * tn, (c + 1) * tn)
        raw_ref[:, cols] = jnp.dot(y_ref[:, :D_ATTN], wo_ref[:D_ATTN, cols],
                                   preferred_element_type=F32)

    def pool_half(raw_ref, c):
        cols = slice(c * tn, (c + 1) * tn)
        raw_ref[:, cols] += jnp.dot(y_ref[:, D_ATTN:], wo_ref[D_ATTN:, cols],
                                    preferred_element_type=F32)

    def pool_group(grp):
        window = POOL_WINDOWS[grp]
        i = jnp.minimum(g, n_tiles - 1) % tiles_per_seq
        pos = i * tm + lax.broadcasted_iota(jnp.int32, (tm, 1), 0)
        inv_count = 1.0 / jnp.minimum(pos + 1, window).astype(F32)
        pooled = []
        for k in range(grp * slabs_per_group, (grp + 1) * slabs_per_group):
            u = u_ref[k].astype(F32)
            halo = jnp.where(i > 0, uh_ref[k].astype(F32), 0.0)
            tot = jnp.concatenate([halo, u], axis=0)
            d = 1
            while d < window:
                tot = tot + pltpu.roll(tot, d, 0)
                d *= 2
            pooled.append((tot[POOL_HALO:] * inv_count - u).astype(BF16))
        po = jnp.dot(jnp.concatenate(pooled, axis=1), wp_ref[grp], preferred_element_type=F32)
        for n, k in enumerate(range(grp * slabs_per_group, (grp + 1) * slabs_per_group)):
            cols = slice(k * LANES, (k + 1) * LANES)
            yk = po[:, n * LANES:(n + 1) * LANES] * ps_ref[:, cols] * gp_ref[k].astype(F32)
            y_ref[:, D_ATTN + k * LANES:D_ATTN + (k + 1) * LANES] = yk.astype(BF16)

    def runtime_zero(v):
        return pltpu.bitcast(
            lax.shift_right_logical(pltpu.bitcast(v, jnp.uint32), jnp.uint32(32)), F32)

    def normalise(raw_ref, chunk, after=None):
        rows = slice(chunk * rc, (chunk + 1) * rc)
        z = DEEPNORM_ALPHA * x_ref[rows, :] + raw_ref[rows, :]
        mu = jnp.mean(z, axis=-1, keepdims=True)
        if after is not None:
            mu = mu + runtime_zero(after)[:, :1]
        zc = z - mu
        var = jnp.mean(zc * zc, axis=-1, keepdims=True)
        o = zc * lax.rsqrt(var + LN_EPS) * gain_ref[...] + bias_ref[...]
        out_ref[rows, :] = o
        fold = o[:, :LANES]
        for j in range(1, D_MODEL // LANES):
            fold = fold + o[:, j * LANES:(j + 1) * LANES]
        return fold

    def step(raw_cur, raw_prev):
        norm_chunks = iter(range(n_norm))
        spots = n_col - 1
        groups = [n_norm // spots + (1 if c < n_norm % spots else 0) for c in range(spots)]

        def norm_group(count):
            done = None
            for _ in range(count):
                fold = normalise(raw_prev, next(norm_chunks))
                done = fold if done is None else done + fold
            tile = y_ref[:rc, D_ATTN:D_ATTN + LANES]
            y_ref[:rc, D_ATTN:D_ATTN + LANES] = (
                tile.astype(F32) + runtime_zero(done)).astype(BF16)

        if raw_cur is not None:
            for k in range(N_HEADS):
                y_ref[:, k * LANES:(k + 1) * LANES] = ya_ref[k]
            for c in range(n_col):
                attn_half(raw_cur, c)
                pool_group(c)
            for c in range(n_col):
                pool_half(raw_cur, c)
                if raw_prev is not None and c < spots:
                    norm_group(groups[c])
        else:
            for chunk in norm_chunks:
                normalise(raw_prev, chunk)

    assert len(POOL_WINDOWS) == n_col
    interior = (g >= 1) & (g < n_tiles)
    for parity in (0, 1):
        @pl.when(interior & (g % 2 == parity))
        def _():
            step(raw[parity], raw[1 - parity])

    @pl.when(g == 0)
    def _():
        step(raw[0], None)

    @pl.when(g == n_tiles)
    def _():
        step(None, raw[(n_tiles - 1) % 2])


def _output(y_attn, slabs, w_pool_b, pool_scale, w_out_b, x2, gain, bias, batch, seq):
    tm = OUT_TM
    tiles_per_seq = seq // tm
    n_tiles = batch * tiles_per_seq
    halo_per_tile = tm // POOL_HALO
    front = lambda g: jnp.minimum(g, n_tiles - 1)
    back = lambda g: jnp.maximum(g - 1, 0)
    const = lambda shape: pl.BlockSpec(shape, lambda g: (0,) * len(shape),
                                       pipeline_mode=pl.Buffered(1))
    group = lambda first: pl.BlockSpec((POOL_SLABS, tm, LANES),
                                       lambda g: (first // POOL_SLABS, front(g), 0))
    kern = functools.partial(_out_kernel, tm=tm, n_tiles=n_tiles, tiles_per_seq=tiles_per_seq)
    return pl.pallas_call(
        kern,
        grid=(n_tiles + 1,),
        in_specs=[
            group(0),
            group(SLAB_U),
            pl.BlockSpec((POOL_SLABS, POOL_HALO, LANES),
                         lambda g: (SLAB_U // POOL_SLABS,
                                    jnp.maximum(front(g) * halo_per_tile - 1, 0), 0)),
            group(SLAB_GATE_POOL),
            const((len(POOL_WINDOWS), POOL_GROUP_DIM, POOL_GROUP_DIM)),
            const((1, D_POOL)),
            const((D_MODEL, D_MODEL)),
            pl.BlockSpec((tm, D_MODEL), lambda g: (back(g), 0)),
            const((1, D_MODEL)),
            const((1, D_MODEL)),
        ],
        out_specs=pl.BlockSpec((tm, D_MODEL), lambda g: (back(g), 0)),
        out_shape=jax.ShapeDtypeStruct((batch * seq, D_MODEL), F32),
        scratch_shapes=[
            pltpu.VMEM((tm, D_MODEL), BF16),
            pltpu.VMEM((2, tm, D_MODEL), F32),
        ],
        compiler_params=pltpu.CompilerParams(
            dimension_semantics=("arbitrary",),
            vmem_limit_bytes=VMEM_LIMIT),
        name="pool_out_deepnorm",
    )(y_attn, slabs, slabs, slabs, w_pool_b, pool_scale, w_out_b, x2, gain, bias)


def _rotary_tables(seq):
    inv_freq = ROPE_THETA ** (-(2.0 * np.arange(ROPE_HALF, dtype=np.float64)) / ROPE_DIM)
    ang = np.arange(seq, dtype=np.float64)[:, None] * inv_freq[None, :]
    cos, sin = np.cos(ang), np.sin(ang)
    zeros = np.zeros((seq, HEAD_DIM - ROPE_DIM))
    zh = np.zeros((seq, ROPE_HALF))
    c = np.concatenate([cos, cos, np.ones_like(zeros)], axis=1)
    a = np.concatenate([zh, sin, zeros], axis=1)
    b = np.concatenate([-sin, zh, zeros], axis=1)
    return jnp.asarray(np.stack([c, a, b]), dtype=F32)


def kernel(x, w_in, w_pool, pool_scale, w_out, ln_gain, ln_bias):
    batch, seq, d_model = x.shape
    assert d_model == D_MODEL and seq % (MAX_DIL * N_KEYS) == 0 and seq % PROJ_TM == 0
    assert w_in.shape[0] == 1, "single layer"
    x2 = x.reshape(batch * seq, D_MODEL)
    rot = _rotary_tables(seq)
    slabs, qkv16 = _projection(x2, w_in[0].astype(BF16), rot, batch, seq)
    qkv16 = qkv16.reshape(3 * N_HEADS, batch, seq, LANES)
    y_attn = _attention(slabs, qkv16, batch, seq)
    out = _output(y_attn, slabs, w_pool[0].astype(BF16), pool_scale.astype(F32),
                  w_out[0].astype(BF16), x2, ln_gain.astype(F32), ln_bias.astype(F32),
                  batch, seq)
    return out.reshape(batch, seq, D_MODEL)
```

```python
import functools

import jax
import jax.numpy as jnp
import numpy as np
from jax import lax
from jax.experimental import pallas as pl
from jax.experimental.pallas import tpu as pltpu

D_MODEL = 2048
D_ATTN = 1024
D_POOL = 1024
HEAD_DIM = 128
N_HEADS = D_ATTN // HEAD_DIM
ROPE_DIM = HEAD_DIM // 4
ROPE_HALF = ROPE_DIM // 2
ROPE_THETA = 500000.0
N_KEYS = 128
MAX_DIL = 16
POOL_WINDOWS = (2, 4, 8, 16)
POOL_GROUP_DIM = D_POOL // len(POOL_WINDOWS)
POOL_HALO = 16
D_IN = 3 * D_ATTN + D_POOL + D_MODEL
LN_EPS = 1e-5
DEEPNORM_ALPHA = 2.0 ** 0.25
Q_SCALE = HEAD_DIM ** -0.5 * float(np.log2(np.e))

LANES = 128
VMEM_LIMIT = 56 * 1024 * 1024

SLAB_Q, SLAB_K, SLAB_V = 0, N_HEADS, 2 * N_HEADS
SLAB_U = 3 * D_ATTN // LANES
SLAB_GATE_ATTN = (3 * D_ATTN + D_POOL) // LANES
SLAB_GATE_POOL = (3 * D_ATTN + D_POOL + D_ATTN) // LANES
N_SLABS = D_IN // LANES
POOL_SLABS = D_POOL // LANES

PROJ_TM = 1024
PROJ_TN = 1024
OUT_TM = 512
ATTN_UNROLL = 32

BF16 = jnp.bfloat16
F32 = jnp.float32


def _proj_kernel(x_ref, w_ref, rot_ref, nat_ref, r16_ref, xb_ref, raw0_ref, raw1_ref, tmp_ref, *,
                 tm, tn, n_tiles):
    s = pl.program_id(0)
    heads = tn // LANES
    n_col = D_IN // tn
    n_q = D_ATTN // tn
    n_qkv = 3 * n_q
    n_gate0 = n_qkv + D_POOL // tn
    sub = tm // MAX_DIL
    col_e = jnp.maximum(s - 1, 0) % n_col

    @pl.when((s % n_col == 0) & (s < n_tiles))
    def _():
        xb_ref[...] = x_ref[...].astype(BF16)

    raw = (raw0_ref, raw1_ref)

    def matmul_to(raw_ref):
        res = jnp.dot(xb_ref[...], w_ref[...], preferred_element_type=F32)
        for h in range(heads):
            raw_ref[h] = res[:, h * LANES:(h + 1) * LANES]

    def qkv_epilogue(raw_ref):
        is_v = col_e >= 2 * n_q
        q_scale = jnp.where(col_e < n_q, Q_SCALE, 1.0).astype(F32)
        rc = 128
        for c0 in range(0, tm, rc):
            rows = slice(c0, c0 + rc)
            ct = rot_ref[0, rows, :] * q_scale
            at = rot_ref[1, rows, :] * q_scale
            bt = rot_ref[2, rows, :] * q_scale
            for h in range(heads):
                t = raw_ref[h, rows, :]
                rot = (t * ct + pltpu.roll(t, ROPE_HALF, 1) * at
                       + pltpu.roll(t, LANES - ROPE_HALF, 1) * bt)
                t = jnp.where(is_v, t, rot)
                raw_ref[h, rows, :] = t
                nat_ref[h, rows, :] = t.astype(BF16)
        quarter = tm // 4
        for h in range(heads):
            for r4 in range(4):
                tmp_ref[h, r4 * quarter:(r4 + 1) * quarter, :] = (
                    raw_ref[h, pl.ds(r4, quarter, stride=4), :])
        for r4 in range(4):
            for c in range(4):
                for h in range(heads):
                    t = tmp_ref[h, pl.ds(r4 * quarter + c, sub, stride=4), :]
                    r16_ref[h, 4 * c + r4] = t.astype(BF16)

    def rest_epilogue(raw_ref):
        is_gate = col_e >= n_gate0
        for h in range(heads):
            g = raw_ref[h]
            silu = g / (1.0 + jnp.exp(-g))
            nat_ref[h] = jnp.where(is_gate, silu, g).astype(BF16)

    interior = (s >= 1) & (s < n_tiles)
    for parity in (0, 1):
        here = interior & (s % 2 == parity)

        @pl.when(here & (col_e < n_qkv))
        def _():
            qkv_epilogue(raw[1 - parity])
            matmul_to(raw[parity])

        @pl.when(here & (col_e >= n_qkv))
        def _():
            rest_epilogue(raw[1 - parity])
            matmul_to(raw[parity])

    @pl.when(s == 0)
    def _():
        matmul_to(raw[0])

    @pl.when(s == n_tiles)
    def _():
        rest_epilogue(raw[(n_tiles - 1) % 2])


def _projection(x2, w_tiles, rot, batch, seq):
    tm, tn = PROJ_TM, PROJ_TN
    m = x2.shape[0]
    heads = tn // LANES
    tiles_per_seq = seq // tm
    n_col = D_IN // tn
    n_q = D_ATTN // tn
    last_qkv = 3 * n_q - 1
    n_tiles = (m // tm) * n_col
    assert n_col % 2 == 0
    kern = functools.partial(_proj_kernel, tm=tm, tn=tn, n_tiles=n_tiles)

    def mm_tile(s):
        t = jnp.minimum(s, n_tiles - 1)
        return t // n_col, t % n_col

    def ep_tile(s):
        t = jnp.maximum(s - 1, 0)
        return t // n_col, t % n_col

    def rot_map(s):
        return (0, ep_tile(s)[0] % tiles_per_seq, 0)

    def nat_map(s):
        i, j = ep_tile(s)
        return (j, i, 0)

    def r16_map(s):
        i, j = ep_tile(s)
        return (jnp.minimum(j, last_qkv), i // tiles_per_seq, 0, i % tiles_per_seq, 0)

    return pl.pallas_call(
        kern,
        grid=(n_tiles + 1,),
        in_specs=[
            pl.BlockSpec((tm, D_MODEL), lambda s: (mm_tile(s)[0], 0)),
            pl.BlockSpec((D_MODEL, tn), lambda s: (0, mm_tile(s)[1])),
            pl.BlockSpec((3, tm, LANES), rot_map),
        ],
        out_specs=[
            pl.BlockSpec((heads, tm, LANES), nat_map),
            pl.BlockSpec((heads, None, MAX_DIL, tm // MAX_DIL, LANES), r16_map),
        ],
        out_shape=[
            jax.ShapeDtypeStruct((N_SLABS, m, LANES), BF16),
            jax.ShapeDtypeStruct((3 * N_HEADS, batch, MAX_DIL, seq // MAX_DIL, LANES), BF16),
        ],
        scratch_shapes=[
            pltpu.VMEM((tm, D_MODEL), BF16),
            pltpu.VMEM((heads, tm, LANES), F32),
            pltpu.VMEM((heads, tm, LANES), F32),
            pltpu.VMEM((heads, tm, LANES), F32),
        ],
        compiler_params=pltpu.CompilerParams(
            dimension_semantics=("arbitrary",),
            vmem_limit_bytes=VMEM_LIMIT),
        name="in_projection",
    )(x2, w_tiles, rot)


def _attend(q, k, vx, bias):
    s = lax.dot_general(q, k, (((1,), (1,)), ((), ())), preferred_element_type=F32)
    s = s.astype(BF16) + bias
    m = jnp.max(s, axis=1, keepdims=True)
    p = jnp.exp2(s - m)
    acc = jnp.dot(p, vx, preferred_element_type=F32)
    den = acc[:, HEAD_DIM:]
    return acc[:, :HEAD_DIM] / den, m.astype(F32) + jnp.log2(den)


def _attn_kernel(q_ref, k_ref, v_ref, g_ref, q16_ref, k16_ref, v16_ref, bias_ref, out_ref,
                 o1_ref, l1_ref, o16_ref, l16_ref, onat_ref, vx_ref, vx16_ref, *, seq):
    W = N_KEYS
    n_sub16 = seq // MAX_DIL
    blocks = seq // W
    blocks16 = n_sub16 // W
    blocks4 = seq // 4 // W
    chunk4 = W // 4
    step = 512

    def fill_vx(t, carry):
        rows = pl.ds(pl.multiple_of(t * step, step), step)
        ones = jnp.ones((step, HEAD_DIM), BF16)
        vx_ref[rows, :HEAD_DIM] = v_ref[rows, :]
        vx_ref[rows, HEAD_DIM:] = ones
        vx16_ref[rows, :HEAD_DIM] = v16_ref[rows, :]
        vx16_ref[rows, HEAD_DIM:] = ones
        return carry

    lax.fori_loop(0, seq // step, fill_vx, 0)

    def banded_pass(qr, kr, vr, o_ref, l_ref, blocks_per_seq):
        def body(it, carry):
            for u in range(ATTN_UNROLL):
                g = it * ATTN_UNROLL + u
                kstart = pl.multiple_of(jnp.maximum(g - 1, 0) * W, W)
                first = (g % blocks_per_seq) == 0
                bidx = jnp.where(g == 0, 2, jnp.where(first, 1, 0))
                qrows = pl.ds(pl.multiple_of(g * W, W), W)
                krows = pl.ds(kstart, 2 * W)
                o, lse = _attend(qr[qrows, :], kr[krows, :], vr[krows, :], bias_ref[bidx])
                o_ref[qrows, :] = o
                l_ref[qrows, :] = lse
            return carry

        lax.fori_loop(0, blocks // ATTN_UNROLL, body, 0)

    banded_pass(q_ref, k_ref, vx_ref, o1_ref, l1_ref, blocks)
    banded_pass(q16_ref, k16_ref, vx16_ref, o16_ref, l16_ref, blocks16)

    def d4_merge_store(r4, j, o4, lse4):
        for c in range(4):
            cls = 4 * c + r4
            rows16 = pl.ds(pl.multiple_of(cls * n_sub16 + j * chunk4, chunk4), chunk4)
            rows_nat = pl.ds(j * chunk4 * MAX_DIL + cls, chunk4, stride=MAX_DIL)
            oc4 = o4[c * chunk4:(c + 1) * chunk4, :]
            lc4 = lse4[c * chunk4:(c + 1) * chunk4, :]
            oc16, lc16 = o16_ref[rows16, :], l16_ref[rows16, :]
            oc1, lc1 = o1_ref[rows_nat, :], l1_ref[rows_nat, :]
            mx = jnp.maximum(jnp.maximum(lc1, lc4), lc16)
            w1, w4, w16 = jnp.exp2(lc1 - mx), jnp.exp2(lc4 - mx), jnp.exp2(lc16 - mx)
            mixed = (w1 * oc1 + w4 * oc4 + w16 * oc16) / (w1 + w4 + w16)
            onat_ref[rows_nat, :] = mixed

    def d4_body(it, carry):
        for u in range(ATTN_UNROLL):
            g = it * ATTN_UNROLL + u
            r4 = g // blocks4
            j = g % blocks4
            qs = pl.multiple_of(j * chunk4, chunk4)
            ks = pl.multiple_of(jnp.maximum(j - 1, 0) * chunk4, chunk4)

            def gather(ref, start, size):
                return jnp.concatenate(
                    [ref[pl.ds(pl.multiple_of((4 * c + r4) * n_sub16 + start, chunk4), size), :]
                     for c in range(4)], axis=0)

            bidx = jnp.where(j == 0, 4, 3)
            o4, lse4 = _attend(gather(q16_ref, qs, chunk4), gather(k16_ref, ks, 2 * chunk4),
                               gather(vx16_ref, ks, 2 * chunk4), bias_ref[bidx])
            d4_merge_store(r4, j, o4, lse4)
        return carry

    lax.fori_loop(0, blocks // ATTN_UNROLL, d4_body, 0)

    def out_body(t, carry):
        rows = pl.ds(pl.multiple_of(t * step, step), step)
        out_ref[rows, :] = (onat_ref[rows, :] * g_ref[rows, :].astype(F32)).astype(BF16)
        return carry

    lax.fori_loop(0, seq // step, out_body, 0)


def _attention_biases():
    W = N_KEYS
    ninf = -jnp.inf
    qi = jnp.arange(W)[:, None]
    kj = jnp.arange(2 * W)[None, :]
    dist = W + qi - kj
    full = (dist >= 0) & (dist <= W)
    first = (dist >= 0) & (kj >= W)
    first0 = kj <= qi
    cq = W // 4
    mq = 4 * (qi % cq) + qi // cq
    mk = 4 * (kj % (2 * cq) - cq) + kj // (2 * cq)
    full4 = (mq - mk >= 0) & (mq - mk <= W)
    mk0 = 4 * (kj % (2 * cq)) + kj // (2 * cq)
    first4 = mk0 <= mq
    masks = jnp.stack([full, first, first0, full4, first4])
    return jnp.where(masks, 0.0, ninf).astype(BF16)


def _attention(slabs, qkv16, batch, seq):
    bias = _attention_biases()
    nat = lambda off: pl.BlockSpec((None, seq, LANES), lambda b, h: (off + h, b, 0))
    r16 = lambda off: pl.BlockSpec((None, None, seq, LANES), lambda b, h: (off + h, b, 0, 0))
    kern = functools.partial(_attn_kernel, seq=seq)
    return pl.pallas_call(
        kern,
        grid=(batch, N_HEADS),
        in_specs=[nat(SLAB_Q), nat(SLAB_K), nat(SLAB_V), nat(SLAB_GATE_ATTN),
                  r16(SLAB_Q), r16(SLAB_K), r16(SLAB_V),
                  pl.BlockSpec((5, N_KEYS, 2 * N_KEYS), lambda b, h: (0, 0, 0))],
        out_specs=pl.BlockSpec((None, seq, LANES), lambda b, h: (h, b, 0)),
        out_shape=jax.ShapeDtypeStruct((N_HEADS, batch * seq, LANES), BF16),
        scratch_shapes=[pltpu.VMEM((seq, LANES), F32) for _ in range(5)]
        + [pltpu.VMEM((seq, 2 * HEAD_DIM), BF16) for _ in range(2)],
        compiler_params=pltpu.CompilerParams(
            dimension_semantics=("arbitrary", "arbitrary"),
            vmem_limit_bytes=VMEM_LIMIT),
        name="dilated_attention",
    )(slabs, slabs, slabs, slabs, qkv16, qkv16, qkv16, bias)


def _out_kernel(ya_ref, u_ref, uh_ref, gp_ref, wp_ref, ps_ref, wo_ref, x_ref, gain_ref,
                bias_ref, out_ref, y_ref, raw_ref, *, tm, n_tiles, tiles_per_seq):
    g = pl.program_id(0)
    raw = (raw_ref.at[0], raw_ref.at[1])
    slabs_per_group = POOL_GROUP_DIM // LANES

    n_col = 4
    tn = D_MODEL // n_col
    rc = 16
    n_norm = tm // rc

    def attn_half(raw_ref, c):
        cols = slice(c * tn, (c + 1) * tn)
        raw_ref[:, cols] = jnp.dot(y_ref[:, :D_ATTN], wo_ref[:D_ATTN, cols],
                                   preferred_element_type=F32)

    def pool_half(raw_ref, c):
        cols = slice(c * tn, (c + 1) * tn)
        raw_ref[:, cols] += jnp.dot(y_ref[:, D_ATTN:], wo_ref[D_ATTN:, cols],
                                    preferred_element_type=F32)

    def pool_group(grp):
        window = POOL_WINDOWS[grp]
        i = jnp.minimum(g, n_tiles - 1) % tiles_per_seq
        pos = i * tm + lax.broadcasted_iota(jnp.int32, (tm, 1), 0)
        inv_count = 1.0 / jnp.minimum(pos + 1, window).astype(F32)
        pooled = []
        for k in range(grp * slabs_per_group, (grp + 1) * slabs_per_group):
            u = u_ref[k].astype(F32)
            halo = jnp.where(i > 0, uh_ref[k].astype(F32), 0.0)
            tot = jnp.concatenate([halo, u], axis=0)
            d = 1
            while d < window:
                tot = tot + pltpu.roll(tot, d, 0)
                d *= 2
            pooled.append((tot[POOL_HALO:] * inv_count - u).astype(BF16))
        po = jnp.dot(jnp.concatenate(pooled, axis=1), wp_ref[grp], preferred_element_type=F32)
        for n, k in enumerate(range(grp * slabs_per_group, (grp + 1) * slabs_per_group)):
            cols = slice(k * LANES, (k + 1) * LANES)
            yk = po[:, n * LANES:(n + 1) * LANES] * ps_ref[:, cols] * gp_ref[k].astype(F32)
            y_ref[:, D_ATTN + k * LANES:D_ATTN + (k + 1) * LANES] = yk.astype(BF16)

    def runtime_zero(v):
        return pltpu.bitcast(
            lax.shift_right_logical(pltpu.bitcast(v, jnp.uint32), jnp.uint32(32)), F32)

    def normalise(raw_ref, chunk, after=None):
        rows = slice(chunk * rc, (chunk + 1) * rc)
        z = DEEPNORM_ALPHA * x_ref[rows, :] + raw_ref[rows, :]
        mu = jnp.mean(z, axis=-1, keepdims=True)
        if after is not None:
            mu = mu + runtime_zero(after)[:, :1]
        zc = z - mu
        var = jnp.mean(zc * zc, axis=-1, keepdims=True)
        o = zc * lax.rsqrt(var + LN_EPS) * gain_ref[...] + bias_ref[...]
        out_ref[rows, :] = o
        fold = o[:, :LANES]
        for j in range(1, D_MODEL // LANES):
            fold = fold + o[:, j * LANES:(j + 1) * LANES]
        return fold

    def step(raw_cur, raw_prev):
        norm_chunks = iter(range(n_norm))
        spots = n_col - 1
        groups = [n_norm // spots + (1 if c < n_norm % spots else 0) for c in range(spots)]

        def norm_group(count):
            done = None
            for _ in range(count):
                fold = normalise(raw_prev, next(norm_chunks))
                done = fold if done is None else done + fold
            tile = y_ref[:rc, D_ATTN:D_ATTN + LANES]
            y_ref[:rc, D_ATTN:D_ATTN + LANES] = (
                tile.astype(F32) + runtime_zero(done)).astype(BF16)

        if raw_cur is not None:
            for k in range(N_HEADS):
                y_ref[:, k * LANES:(k + 1) * LANES] = ya_ref[k]
            for c in range(n_col):
                attn_half(raw_cur, c)
                pool_group(c)
            for c in range(n_col):
                pool_half(raw_cur, c)
                if raw_prev is not None and c < spots:
                    norm_group(groups[c])
        else:
            for chunk in norm_chunks:
                normalise(raw_prev, chunk)

    assert len(POOL_WINDOWS) == n_col
    interior = (g >= 1) & (g < n_tiles)
    for parity in (0, 1):
        @pl.when(interior & (g % 2 == parity))
        def _():
            step(raw[parity], raw[1 - parity])

    @pl.when(g == 0)
    def _():
        step(raw[0], None)

    @pl.when(g == n_tiles)
    def _():
        step(None, raw[(n_tiles - 1) % 2])


def _output(y_attn, slabs, w_pool_b, pool_scale, w_out_b, x2, gain, bias, batch, seq):
    tm = OUT_TM
    tiles_per_seq = seq // tm
    n_tiles = batch * tiles_per_seq
    halo_per_tile = tm // POOL_HALO
    front = lambda g: jnp.minimum(g, n_tiles - 1)
    back = lambda g: jnp.maximum(g - 1, 0)
    const = lambda shape: pl.BlockSpec(shape, lambda g: (0,) * len(shape),
                                       pipeline_mode=pl.Buffered(1))
    group = lambda first: pl.BlockSpec((POOL_SLABS, tm, LANES),
                                       lambda g: (first // POOL_SLABS, front(g), 0))
    kern = functools.partial(_out_kernel, tm=tm, n_tiles=n_tiles, tiles_per_seq=tiles_per_seq)
    return pl.pallas_call(
        kern,
        grid=(n_tiles + 1,),
        in_specs=[
            group(0),
            group(SLAB_U),
            pl.BlockSpec((POOL_SLABS, POOL_HALO, LANES),
                         lambda g: (SLAB_U // POOL_SLABS,
                                    jnp.maximum(front(g) * halo_per_tile - 1, 0), 0)),
            group(SLAB_GATE_POOL),
            const((len(POOL_WINDOWS), POOL_GROUP_DIM, POOL_GROUP_DIM)),
            const((1, D_POOL)),
            const((D_MODEL, D_MODEL)),
            pl.BlockSpec((tm, D_MODEL), lambda g: (back(g), 0)),
            const((1, D_MODEL)),
            const((1, D_MODEL)),
        ],
        out_specs=pl.BlockSpec((tm, D_MODEL), lambda g: (back(g), 0)),
        out_shape=jax.ShapeDtypeStruct((batch * seq, D_MODEL), F32),
        scratch_shapes=[
            pltpu.VMEM((tm, D_MODEL), BF16),
            pltpu.VMEM((2, tm, D_MODEL), F32),
        ],
        compiler_params=pltpu.CompilerParams(
            dimension_semantics=("arbitrary",),
            vmem_limit_bytes=VMEM_LIMIT),
        name="pool_out_deepnorm",
    )(y_attn, slabs, slabs, slabs, w_pool_b, pool_scale, w_out_b, x2, gain, bias)


def _rotary_tables(seq):
    inv_freq = ROPE_THETA ** (-(2.0 * np.arange(ROPE_HALF, dtype=np.float64)) / ROPE_DIM)
    ang = np.arange(seq, dtype=np.float64)[:, None] * inv_freq[None, :]
    cos, sin = np.cos(ang), np.sin(ang)
    zeros = np.zeros((seq, HEAD_DIM - ROPE_DIM))
    zh = np.zeros((seq, ROPE_HALF))
    c = np.concatenate([cos, cos, np.ones_like(zeros)], axis=1)
    a = np.concatenate([zh, sin, zeros], axis=1)
    b = np.concatenate([-sin, zh, zeros], axis=1)
    return jnp.asarray(np.stack([c, a, b]), dtype=F32)


def kernel(x, w_in, w_pool, pool_scale, w_out, ln_gain, ln_bias):
    batch, seq, d_model = x.shape
    assert d_model == D_MODEL and seq % (MAX_DIL * N_KEYS) == 0 and seq % PROJ_TM == 0
    assert w_in.shape[0] == 1, "single layer"
    x2 = x.reshape(batch * seq, D_MODEL)
    rot = _rotary_tables(seq)
    slabs, qkv16 = _projection(x2, w_in[0].astype(BF16), rot, batch, seq)
    qkv16 = qkv16.reshape(3 * N_HEADS, batch, seq, LANES)
    y_attn = _attention(slabs, qkv16, batch, seq)
    out = _output(y_attn, slabs, w_pool[0].astype(BF16), pool_scale.astype(F32),
                  w_out[0].astype(BF16), x2, ln_gain.astype(F32), ln_bias.astype(F32),
                  batch, seq)
    return out.reshape(batch, seq, D_MODEL)
```

```python
import functools

import jax
import jax.numpy as jnp
import numpy as np
from jax import lax
from jax.experimental import pallas as pl
from jax.experimental.pallas import tpu as pltpu

D_MODEL = 2048
D_ATTN = 1024
D_POOL = 1024
HEAD_DIM = 128
N_HEADS = D_ATTN // HEAD_DIM
ROPE_DIM = HEAD_DIM // 4
ROPE_HALF = ROPE_DIM // 2
ROPE_THETA = 500000.0
N_KEYS = 128
MAX_DIL = 16
POOL_WINDOWS = (2, 4, 8, 16)
POOL_GROUP_DIM = D_POOL // len(POOL_WINDOWS)
POOL_HALO = 16
D_IN = 3 * D_ATTN + D_POOL + D_MODEL
LN_EPS = 1e-5
DEEPNORM_ALPHA = 2.0 ** 0.25
Q_SCALE = HEAD_DIM ** -0.5 * float(np.log2(np.e))

LANES = 128
VMEM_LIMIT = 56 * 1024 * 1024

SLAB_Q, SLAB_K, SLAB_V = 0, N_HEADS, 2 * N_HEADS
SLAB_U = 3 * D_ATTN // LANES
SLAB_GATE_ATTN = (3 * D_ATTN + D_POOL) // LANES
SLAB_GATE_POOL = (3 * D_ATTN + D_POOL + D_ATTN) // LANES
N_SLABS = D_IN // LANES
POOL_SLABS = D_POOL // LANES

PROJ_TM = 1024
PROJ_TN = 1024
OUT_TM = 512

BF16 = jnp.bfloat16
F32 = jnp.float32


def _proj_kernel(x_ref, w_ref, rot_ref, nat_ref, r16_ref, xb_ref, raw0_ref, raw1_ref, tmp_ref, *,
                 tm, tn, n_tiles):
    s = pl.program_id(0)
    heads = tn // LANES
    n_col = D_IN // tn
    n_q = D_ATTN // tn
    n_qkv = 3 * n_q
    n_gate0 = n_qkv + D_POOL // tn
    sub = tm // MAX_DIL
    col_e = jnp.maximum(s - 1, 0) % n_col

    @pl.when((s % n_col == 0) & (s < n_tiles))
    def _():
        xb_ref[...] = x_ref[...].astype(BF16)

    raw = (raw0_ref, raw1_ref)

    def matmul_to(raw_ref):
        res = jnp.dot(xb_ref[...], w_ref[...], preferred_element_type=F32)
        for h in range(heads):
            raw_ref[h] = res[:, h * LANES:(h + 1) * LANES]

    def qkv_epilogue(raw_ref):
        is_v = col_e >= 2 * n_q
        q_scale = jnp.where(col_e < n_q, Q_SCALE, 1.0).astype(F32)
        rc = 128
        for c0 in range(0, tm, rc):
            rows = slice(c0, c0 + rc)
            ct = rot_ref[0, rows, :] * q_scale
            at = rot_ref[1, rows, :] * q_scale
            bt = rot_ref[2, rows, :] * q_scale
            for h in range(heads):
                t = raw_ref[h, rows, :]
                rot = (t * ct + pltpu.roll(t, ROPE_HALF, 1) * at
                       + pltpu.roll(t, LANES - ROPE_HALF, 1) * bt)
                t = jnp.where(is_v, t, rot)
                raw_ref[h, rows, :] = t
                nat_ref[h, rows, :] = t.astype(BF16)
        quarter = tm // 4
        for h in range(heads):
            for r4 in range(4):
                tmp_ref[h, r4 * quarter:(r4 + 1) * quarter, :] = (
                    raw_ref[h, pl.ds(r4, quarter, stride=4), :])
        for r4 in range(4):
            for c in range(4):
                for h in range(heads):
                    t = tmp_ref[h, pl.ds(r4 * quarter + c, sub, stride=4), :]
                    r16_ref[h, 4 * c + r4] = t.astype(BF16)

    def rest_epilogue(raw_ref):
        is_gate = col_e >= n_gate0
        for h in range(heads):
            g = raw_ref[h]
            silu = g / (1.0 + jnp.exp(-g))
            nat_ref[h] = jnp.where(is_gate, silu, g).astype(BF16)

    interior = (s >= 1) & (s < n_tiles)
    for parity in (0, 1):
        here = interior & (s % 2 == parity)

        @pl.when(here & (col_e < n_qkv))
        def _():
            qkv_epilogue(raw[1 - parity])
            matmul_to(raw[parity])

        @pl.when(here & (col_e >= n_qkv))
        def _():
            rest_epilogue(raw[1 - parity])
            matmul_to(raw[parity])

    @pl.when(s == 0)
    def _():
        matmul_to(raw[0])

    @pl.when(s == n_tiles)
    def _():
        rest_epilogue(raw[(n_tiles - 1) % 2])


def _projection(x2, w_tiles, rot, batch, seq):
    tm, tn = PROJ_TM, PROJ_TN
    m = x2.shape[0]
    heads = tn // LANES
    tiles_per_seq = seq // tm
    n_col = D_IN // tn
    n_q = D_ATTN // tn
    last_qkv = 3 * n_q - 1
    n_tiles = (m // tm) * n_col
    assert n_col % 2 == 0
    kern = functools.partial(_proj_kernel, tm=tm, tn=tn, n_tiles=n_tiles)

    def mm_tile(s):
        t = jnp.minimum(s, n_tiles - 1)
        return t // n_col, t % n_col

    def ep_tile(s):
        t = jnp.maximum(s - 1, 0)
        return t // n_col, t % n_col

    def rot_map(s):
        return (0, ep_tile(s)[0] % tiles_per_seq, 0)

    def nat_map(s):
        i, j = ep_tile(s)
        return (j, i, 0)

    def r16_map(s):
        i, j = ep_tile(s)
        return (jnp.minimum(j, last_qkv), i // tiles_per_seq, 0, i % tiles_per_seq, 0)

    return pl.pallas_call(
        kern,
        grid=(n_tiles + 1,),
        in_specs=[
            pl.BlockSpec((tm, D_MODEL), lambda s: (mm_tile(s)[0], 0)),
            pl.BlockSpec((D_MODEL, tn), lambda s: (0, mm_tile(s)[1])),
            pl.BlockSpec((3, tm, LANES), rot_map),
        ],
        out_specs=[
            pl.BlockSpec((heads, tm, LANES), nat_map),
            pl.BlockSpec((heads, None, MAX_DIL, tm // MAX_DIL, LANES), r16_map),
        ],
        out_shape=[
            jax.ShapeDtypeStruct((N_SLABS, m, LANES), BF16),
            jax.ShapeDtypeStruct((3 * N_HEADS, batch, MAX_DIL, seq // MAX_DIL, LANES), BF16),
        ],
        scratch_shapes=[
            pltpu.VMEM((tm, D_MODEL), BF16),
            pltpu.VMEM((heads, tm, LANES), F32),
            pltpu.VMEM((heads, tm, LANES), F32),
            pltpu.VMEM((heads, tm, LANES), F32),
        ],
        compiler_params=pltpu.CompilerParams(
            dimension_semantics=("arbitrary",),
            vmem_limit_bytes=VMEM_LIMIT),
        name="in_projection",
    )(x2, w_tiles, rot)


def _attend(q, k, vx, bias):
    s = lax.dot_general(q, k, (((1,), (1,)), ((), ())), preferred_element_type=F32)
    s = s.astype(BF16) + bias
    m = jnp.max(s, axis=1, keepdims=True)
    p = jnp.exp2(s - m)
    acc = jnp.dot(p, vx, preferred_element_type=F32)
    return (acc[:, :HEAD_DIM], acc[:, HEAD_DIM:],
            jnp.broadcast_to(m.astype(F32), (q.shape[0], HEAD_DIM)))


def _attn_kernel(q_ref, k_ref, v_ref, g_ref, q16_ref, k16_ref, v16_ref, bias_ref, out_ref,
                 s1_ref, t1_ref, s16_ref, onat_ref, vx_ref, vx16_ref, *, seq):
    W = N_KEYS
    n_sub16 = seq // MAX_DIL
    blocks = seq // W
    blocks16 = n_sub16 // W
    blocks4 = seq // 4 // W
    chunk4 = W // 4
    step = 512

    def fill_vx(t, carry):
        rows = pl.ds(pl.multiple_of(t * step, step), step)
        ones = jnp.ones((step, HEAD_DIM), BF16)
        vx_ref[rows, :HEAD_DIM] = v_ref[rows, :]
        vx_ref[rows, HEAD_DIM:] = ones
        vx16_ref[rows, :HEAD_DIM] = v16_ref[rows, :]
        vx16_ref[rows, HEAD_DIM:] = ones
        return carry

    lax.fori_loop(0, seq // step, fill_vx, 0)

    def banded_block(qr, kr, vr, state_ref, g, blocks_per_seq):
        kstart = max(g - 1, 0) * W
        bidx = 2 if g == 0 else (1 if g % blocks_per_seq == 0 else 0)
        state = _attend(qr[g * W:(g + 1) * W, :], kr[kstart:kstart + 2 * W, :],
                        vr[kstart:kstart + 2 * W, :], bias_ref[bidx])
        for i, t in enumerate(state):
            state_ref[i, g * W:(g + 1) * W, :] = t

    def d4_block(r4, j):
        def gather(ref, start, size):
            return jnp.concatenate(
                [ref[(4 * c + r4) * n_sub16 + start:(4 * c + r4) * n_sub16 + start + size, :]
                 for c in range(4)], axis=0)

        ks = max(j - 1, 0) * chunk4
        state4 = _attend(gather(q16_ref, j * chunk4, chunk4), gather(k16_ref, ks, 2 * chunk4),
                         gather(vx16_ref, ks, 2 * chunk4), bias_ref[4 if j == 0 else 3])
        base = (r4 * blocks4 + j) * W
        for i in range(3):
            t1_ref[i, base:base + W, :] = s1_ref[i, pl.ds(j * 4 * W + r4, W, stride=4), :]
        for c in range(4):
            start16 = (4 * c + r4) * n_sub16 + j * chunk4
            rows16 = slice(start16, start16 + chunk4)
            rows4 = slice(c * chunk4, (c + 1) * chunk4)
            a4, d4, m4 = (t[rows4, :] for t in state4)
            a16, d16, m16 = (s16_ref[i, rows16, :] for i in range(3))
            a1, d1, m1 = (t1_ref[i, pl.ds(base + c, chunk4, stride=4), :] for i in range(3))
            mx = jnp.maximum(jnp.maximum(m1, m4), m16)
            w1, w4, w16 = jnp.exp2(m1 - mx), jnp.exp2(m4 - mx), jnp.exp2(m16 - mx)
            mixed = (w1 * a1 + w4 * a4 + w16 * a16) / (w1 * d1 + w4 * d4 + w16 * d16)
            onat_ref[pl.ds(j * 4 * W + 4 * c + r4, chunk4, stride=MAX_DIL), :] = mixed

    per16 = blocks4 // blocks16
    for j in range(blocks4):
        if j % per16 == 0:
            for r in range(MAX_DIL):
                banded_block(q16_ref, k16_ref, vx16_ref, s16_ref,
                             r * blocks16 + j // per16, blocks16)
        for g in range(j * blocks // blocks4, (j + 1) * blocks // blocks4):
            banded_block(q_ref, k_ref, vx_ref, s1_ref, g, blocks)
        for r4 in range(4):
            d4_block(r4, j)

    def out_body(t, carry):
        rows = pl.ds(pl.multiple_of(t * step, step), step)
        out_ref[rows, :] = (onat_ref[rows, :] * g_ref[rows, :].astype(F32)).astype(BF16)
        return carry

    lax.fori_loop(0, seq // step, out_body, 0)


def _attention_biases():
    W = N_KEYS
    ninf = -jnp.inf
    qi = jnp.arange(W)[:, None]
    kj = jnp.arange(2 * W)[None, :]
    dist = W + qi - kj
    full = (dist >= 0) & (dist <= W)
    first = (dist >= 0) & (kj >= W)
    first0 = kj <= qi
    cq = W // 4
    mq = 4 * (qi % cq) + qi // cq
    mk = 4 * (kj % (2 * cq) - cq) + kj // (2 * cq)
    full4 = (mq - mk >= 0) & (mq - mk <= W)
    mk0 = 4 * (kj % (2 * cq)) + kj // (2 * cq)
    first4 = mk0 <= mq
    masks = jnp.stack([full, first, first0, full4, first4])
    return jnp.where(masks, 0.0, ninf).astype(BF16)


def _attention(slabs, qkv16, batch, seq):
    bias = _attention_biases()
    nat = lambda off: pl.BlockSpec((None, seq, LANES), lambda b, h: (off + h, b, 0))
    r16 = lambda off: pl.BlockSpec((None, None, seq, LANES), lambda b, h: (off + h, b, 0, 0))
    kern = functools.partial(_attn_kernel, seq=seq)
    return pl.pallas_call(
        kern,
        grid=(batch, N_HEADS),
        in_specs=[nat(SLAB_Q), nat(SLAB_K), nat(SLAB_V), nat(SLAB_GATE_ATTN),
                  r16(SLAB_Q), r16(SLAB_K), r16(SLAB_V),
                  pl.BlockSpec((5, N_KEYS, 2 * N_KEYS), lambda b, h: (0, 0, 0))],
        out_specs=pl.BlockSpec((None, seq, LANES), lambda b, h: (h, b, 0)),
        out_shape=jax.ShapeDtypeStruct((N_HEADS, batch * seq, LANES), BF16),
        scratch_shapes=[pltpu.VMEM((3, seq, LANES), F32) for _ in range(3)]
        + [pltpu.VMEM((seq, LANES), F32)]
        + [pltpu.VMEM((seq, 2 * HEAD_DIM), BF16) for _ in range(2)],
        compiler_params=pltpu.CompilerParams(
            dimension_semantics=("arbitrary", "arbitrary"),
            vmem_limit_bytes=VMEM_LIMIT),
        name="dilated_attention",
    )(slabs, slabs, slabs, slabs, qkv16, qkv16, qkv16, bias)


def _out_kernel(ya_ref, u_ref, uh_ref, gp_ref, wp_ref, ps_ref, wo_ref, x_ref, gain_ref,
                bias_ref, out_ref, y_ref, raw_ref, *, tm, n_tiles, tiles_per_seq):
    g = pl.program_id(0)
    raw = (raw_ref.at[0], raw_ref.at[1])
    slabs_per_group = POOL_GROUP_DIM // LANES

    n_col = 4
    tn = D_MODEL // n_col
    rc = 16
    n_norm = tm // rc

    def attn_half(raw_ref, c):
        cols = slice(c * tn, (c + 1) * tn)
        raw_ref[:, cols] = jnp.dot(y_ref[:, :D_ATTN], wo_ref[:D_ATTN, cols],
                                   preferred_element_type=F32)

    def pool_half(raw_ref, c):
        cols = slice(c * tn, (c + 1) * tn)
        raw_ref[:, cols] += jnp.dot(y_ref[:, D_ATTN:], wo_ref[D_ATTN:, cols],
                                    preferred_element_type=F32)

    def pool_group(grp):
        window = POOL_WINDOWS[grp]
        i = jnp.minimum(g, n_tiles - 1) % tiles_per_seq
        pos = i * tm + lax.broadcasted_iota(jnp.int32, (tm, 1), 0)
        inv_count = 1.0 / jnp.minimum(pos + 1, window).astype(F32)
        pooled = []
        for k in range(grp * slabs_per_group, (grp + 1) * slabs_per_group):
            u = u_ref[k].astype(F32)
            halo = jnp.where(i > 0, uh_ref[k].astype(F32), 0.0)
            tot = jnp.concatenate([halo, u], axis=0)
            d = 1
            while d < window:
                tot = tot + pltpu.roll(tot, d, 0)
                d *= 2
            pooled.append((tot[POOL_HALO:] * inv_count - u).astype(BF16))
        po = jnp.dot(jnp.concatenate(pooled, axis=1), wp_ref[grp], preferred_element_type=F32)
        for n, k in enumerate(range(grp * slabs_per_group, (grp + 1) * slabs_per_group)):
            cols = slice(k * LANES, (k + 1) * LANES)
            yk = po[:, n * LANES:(n + 1) * LANES] * ps_ref[:, cols] * gp_ref[k].astype(F32)
            y_ref[:, D_ATTN + k * LANES:D_ATTN + (k + 1) * LANES] = yk.astype(BF16)

    def runtime_zero(v):
        return pltpu.bitcast(
            lax.shift_right_logical(pltpu.bitcast(v, jnp.uint32), jnp.uint32(32)), F32)

    def normalise(raw_ref, chunk, after=None):
        rows = slice(chunk * rc, (chunk + 1) * rc)
        z = DEEPNORM_ALPHA * x_ref[rows, :] + raw_ref[rows, :]
        mu = jnp.mean(z, axis=-1, keepdims=True)
        if after is not None:
            mu = mu + runtime_zero(after)[:, :1]
        zc = z - mu
        var = jnp.mean(zc * zc, axis=-1, keepdims=True)
        o = zc * lax.rsqrt(var + LN_EPS) * gain_ref[...] + bias_ref[...]
        out_ref[rows, :] = o
        fold = o[:, :LANES]
        for j in range(1, D_MODEL // LANES):
            fold = fold + o[:, j * LANES:(j + 1) * LANES]
        return fold

    def step(raw_cur, raw_prev):
        norm_chunks = iter(range(n_norm))
        spots = n_col - 1
        groups = [n_norm // spots + (1 if c < n_norm % spots else 0) for c in range(spots)]

        def norm_group(count):
            done = None
            for _ in range(count):
                fold = normalise(raw_prev, next(norm_chunks))
                done = fold if done is None else done + fold
            tile = y_ref[:rc, D_ATTN:D_ATTN + LANES]
            y_ref[:rc, D_ATTN:D_ATTN + LANES] = (
                tile.astype(F32) + runtime_zero(done)).astype(BF16)

        if raw_cur is not None:
            for k in range(N_HEADS):
                y_ref[:, k * LANES:(k + 1) * LANES] = ya_ref[k]
            for c in range(n_col):
                attn_half(raw_cur, c)
                pool_group(c)
            for c in range(n_col):
                pool_half(raw_cur, c)
                if raw_prev is not None and c < spots:
                    norm_group(groups[c])
        else:
            for chunk in norm_chunks:
                normalise(raw_prev, chunk)

    assert len(POOL_WINDOWS) == n_col
    interior = (g >= 1) & (g < n_tiles)
    for parity in (0, 1):
        @pl.when(interior & (g % 2 == parity))
        def _():
            step(raw[parity], raw[1 - parity])

    @pl.when(g == 0)
    def _():
        step(raw[0], None)

    @pl.when(g == n_tiles)
    def _():
        step(None, raw[(n_tiles - 1) % 2])


def _output(y_attn, slabs, w_pool_b, pool_scale, w_out_b, x2, gain, bias, batch, seq):
    tm = OUT_TM
    tiles_per_seq = seq // tm
    n_tiles = batch * tiles_per_seq
    halo_per_tile = tm // POOL_HALO
    front = lambda g: jnp.minimum(g, n_tiles - 1)
    back = lambda g: jnp.maximum(g - 1, 0)
    const = lambda shape: pl.BlockSpec(shape, lambda g: (0,) * len(shape),
                                       pipeline_mode=pl.Buffered(1))
    group = lambda first: pl.BlockSpec((POOL_SLABS, tm, LANES),
                                       lambda g: (first // POOL_SLABS, front(g), 0))
    kern = functools.partial(_out_kernel, tm=tm, n_tiles=n_tiles, tiles_per_seq=tiles_per_seq)
    return pl.pallas_call(
        kern,
        grid=(n_tiles + 1,),
        in_specs=[
            group(0),
            group(SLAB_U),
            pl.BlockSpec((POOL_SLABS, POOL_HALO, LANES),
                         lambda g: (SLAB_U // POOL_SLABS,
                                    jnp.maximum(front(g) * halo_per_tile - 1, 0), 0)),
            group(SLAB_GATE_POOL),
            const((len(POOL_WINDOWS), POOL_GROUP_DIM, POOL_GROUP_DIM)),
            const((1, D_POOL)),
            const((D_MODEL, D_MODEL)),
            pl.BlockSpec((tm, D_MODEL), lambda g: (back(g), 0)),
            const((1, D_MODEL)),
            const((1, D_MODEL)),
        ],
        out_specs=pl.BlockSpec((tm, D_MODEL), lambda g: (back(g), 0)),
        out_shape=jax.ShapeDtypeStruct((batch * seq, D_MODEL), F32),
        scratch_shapes=[
            pltpu.VMEM((tm, D_MODEL), BF16),
            pltpu.VMEM((2, tm, D_MODEL), F32),
        ],
        compiler_params=pltpu.CompilerParams(
            dimension_semantics=("arbitrary",),
            vmem_limit_bytes=VMEM_LIMIT),
        name="pool_out_deepnorm",
    )(y_attn, slabs, slabs, slabs, w_pool_b, pool_scale, w_out_b, x2, gain, bias)


def _rotary_tables(seq):
    inv_freq = ROPE_THETA ** (-(2.0 * np.arange(ROPE_HALF, dtype=np.float64)) / ROPE_DIM)
    ang = np.arange(seq, dtype=np.float64)[:, None] * inv_freq[None, :]
    cos, sin = np.cos(ang), np.sin(ang)
    zeros = np.zeros((seq, HEAD_DIM - ROPE_DIM))
    zh = np.zeros((seq, ROPE_HALF))
    c = np.concatenate([cos, cos, np.ones_like(zeros)], axis=1)
    a = np.concatenate([zh, sin, zeros], axis=1)
    b = np.concatenate([-sin, zh, zeros], axis=1)
    return jnp.asarray(np.stack([c, a, b]), dtype=F32)


def kernel(x, w_in, w_pool, pool_scale, w_out, ln_gain, ln_bias):
    batch, seq, d_model = x.shape
    assert d_model == D_MODEL and seq % (MAX_DIL * N_KEYS) == 0 and seq % PROJ_TM == 0
    assert w_in.shape[0] == 1, "single layer"
    x2 = x.reshape(batch * seq, D_MODEL)
    rot = _rotary_tables(seq)
    slabs, qkv16 = _projection(x2, w_in[0].astype(BF16), rot, batch, seq)
    qkv16 = qkv16.reshape(3 * N_HEADS, batch, seq, LANES)
    y_attn = _attention(slabs, qkv16, batch, seq)
    out = _output(y_attn, slabs, w_pool[0].astype(BF16), pool_scale.astype(F32),
                  w_out[0].astype(BF16), x2, ln_gain.astype(F32), ln_bias.astype(F32),
                  batch, seq)
    return out.reshape(batch, seq, D_MODEL)
```
